```python
import jax, jax.numpy as jnp
from jax import lax
import numpy as np

D_MODEL = 1024
BATCH = 8
SEQ = 4096
DEPTH = 1
DEC_BATCH = 32
DEC_SEQ = 32
PAST_LEN = 2048

CHUNK = 64
Q_BLOCK = 128
HEAD_DIM = 64
N_HEADS_A = 8
N_HEADS_B = 8
WIDTH_A = N_HEADS_A * HEAD_DIM
WIDTH_B = N_HEADS_B * HEAD_DIM
IDX_HEADS = 8
IDX_DIM = 64
TOPK_MAX = 256
ROPE_THETA = 500000.0
EPS = 1e-6
NEG_INF = -1e30
D_IN = 4 * WIDTH_A + IDX_HEADS * IDX_DIM + IDX_DIM + IDX_HEADS + 4 * WIDTH_B + 2 * D_MODEL

kernel_name = "dsa_stickbreaking_gated_hybrid_step"


def _split_points():
    sizes = [WIDTH_A] * 4 + [IDX_HEADS * IDX_DIM, IDX_DIM, IDX_HEADS] + [WIDTH_B] * 4 + [D_MODEL, D_MODEL]
    return [int(s) for s in np.cumsum(sizes)[:-1]]


def _rms_norm(x, g):
    xf = x.astype(jnp.float32)
    y = xf * lax.rsqrt(jnp.mean(xf * xf, axis=-1, keepdims=True) + EPS) * g.astype(jnp.float32)
    return y.astype(x.dtype)


def _rope(x, pos):
    rot = x.shape[-1] // 4
    half = rot // 2
    inv_freq = ROPE_THETA ** (-jnp.arange(half, dtype=jnp.float32) / half)
    ang = pos.astype(jnp.float32)[:, None] * inv_freq[None, :]
    cos = jnp.cos(ang)[:, None, :]
    sin = jnp.sin(ang)[:, None, :]
    xr = x[..., :rot].astype(jnp.float32)
    x1, x2 = xr[..., :half], xr[..., half:]
    rotated = jnp.concatenate([x1 * cos - x2 * sin, x2 * cos + x1 * sin], axis=-1).astype(x.dtype)
    return jnp.concatenate([rotated, x[..., rot:]], axis=-1)


def _project(x, pos, norm_g, w_in, q_norm_g, k_norm_g, idx_k_norm_g):
    B, T, _ = x.shape
    xn = _rms_norm(x, norm_g)
    h = jnp.einsum('btd,de->bte', xn, w_in)
    qa, ka, va, ua, qi, ki, wi, qb, kb, vb, ub, ga, gb = jnp.split(h, _split_points(), axis=-1)
    qa = _rope(_rms_norm(qa.reshape(B, T, N_HEADS_A, HEAD_DIM), q_norm_g), pos)
    ka = _rope(_rms_norm(ka.reshape(B, T, N_HEADS_A, HEAD_DIM), k_norm_g), pos)
    va = va.reshape(B, T, N_HEADS_A, HEAD_DIM)
    qi = _rope(qi.reshape(B, T, IDX_HEADS, IDX_DIM), pos)
    ki = _rope(_rms_norm(ki, idx_k_norm_g)[:, :, None, :], pos)[:, :, 0, :]
    wi = wi * (IDX_HEADS ** -0.5)
    qb = qb.reshape(B, T, N_HEADS_B, HEAD_DIM)
    kb = kb.reshape(B, T, N_HEADS_B, HEAD_DIM)
    vb = vb.reshape(B, T, N_HEADS_B, HEAD_DIM)
    return qa, ka, va, ua, qi, ki, wi, qb, kb, vb, ub, ga, gb


def _dsa_block(q, qi, wi, q_pos, k, v, ki, topk):
    S = k.shape[1]
    key_pos = jnp.arange(S, dtype=jnp.int32)
    limit = (q_pos // CHUNK + 1) * CHUNK
    admissible = key_pos[None, :] < limit[:, None]
    dots = jnp.einsum('bqhd,bsd->bqhs', qi.astype(jnp.float32), ki.astype(jnp.float32)) * (IDX_DIM ** -0.5)
    score = jnp.einsum('bqhs,bqh->bqs', jax.nn.relu(dots), wi.astype(jnp.float32))
    score = jnp.where(admissible[None], score, NEG_INF)
    _, idx = lax.top_k(score, topk)
    k_sel = jax.vmap(lambda kk, ii: kk[ii])(k, idx)
    v_sel = jax.vmap(lambda vv, ii: vv[ii])(v, idx)
    ok = idx < limit[None, :, None]
    logits = jnp.einsum('bqhd,bqkhd->bqhk', q.astype(jnp.float32), k_sel.astype(jnp.float32)) * (HEAD_DIM ** -0.5)
    logits = jnp.where(ok[:, :, None, :], logits, NEG_INF)
    p = jax.nn.softmax(logits, axis=-1)
    o = jnp.einsum('bqhk,bqkhd->bqhd', p, v_sel.astype(jnp.float32))
    return o.astype(q.dtype)


def _sb_block(q, q_pos, k, v):
    S = k.shape[1]
    key_pos = jnp.arange(S, dtype=jnp.int32)
    strict = (key_pos[None, :] < q_pos[:, None])[None, None]
    z = jnp.einsum('bqhd,bshd->bhqs', q.astype(jnp.float32), k.astype(jnp.float32)) * (HEAD_DIM ** -0.5)
    log1m = jnp.where(strict, jax.nn.log_sigmoid(-z), 0.0)
    after = lax.cumsum(log1m, axis=3, reverse=True) - log1m
    a = jnp.where(strict, jnp.exp(jax.nn.log_sigmoid(z) + after), 0.0)
    o = jnp.einsum('bhqs,bshd->bqhd', a, v.astype(jnp.float32))
    return o.astype(q.dtype)


def _sweep(fn, q_args, pos):
    B, T = q_args[0].shape[:2]
    nb = T // Q_BLOCK
    blk = lambda a: jnp.moveaxis(a.reshape(B, nb, Q_BLOCK, *a.shape[2:]), 1, 0)
    out = lax.map(lambda args: fn(*args), tuple(blk(a) for a in q_args) + (pos.reshape(nb, Q_BLOCK),))
    return jnp.moveaxis(out, 0, 1).reshape(B, T, *out.shape[3:])


def _layer(x, pos, past, topk, blocked, norm_g, w_in, q_norm_g, k_norm_g, idx_k_norm_g, w_a_out, w_b_out, w_o):
    B, T, _ = x.shape
    qa, ka, va, ua, qi, ki, wi, qb, kb, vb, ub, ga, gb = _project(x, pos, norm_g, w_in, q_norm_g, k_norm_g, idx_k_norm_g)
    new_rows = (ka, va, ki, kb, vb)
    if past is None:
        ka_all, va_all, ki_all, kb_all, vb_all = new_rows
    else:
        ka_all, va_all, ki_all, kb_all, vb_all = [jnp.concatenate([c.astype(n.dtype), n], axis=1) for c, n in zip(past, new_rows)]
    fa = lambda q_, qi_, wi_, p_: _dsa_block(q_, qi_, wi_, p_, ka_all, va_all, ki_all, topk)
    fb = lambda q_, p_: _sb_block(q_, p_, kb_all, vb_all)
    if blocked:
        oa = _sweep(fa, (qa, qi, wi), pos)
        ob = _sweep(fb, (qb,), pos)
    else:
        oa = fa(qa, qi, wi, pos)
        ob = fb(qb, pos)
    ya = jnp.einsum('bte,ed->btd', oa.reshape(B, T, WIDTH_A) * jax.nn.silu(ua), w_a_out)
    yb = jnp.einsum('bte,ed->btd', ob.reshape(B, T, WIDTH_B) * jax.nn.silu(ub), w_b_out)
    mixed = jax.nn.sigmoid(ga) * ya + jax.nn.sigmoid(gb) * yb
    y = x + jnp.einsum('btd,de->bte', mixed, w_o)
    return y, new_rows


def setup_inputs(seed: int = 0) -> dict:
    key = jax.random.key(seed)
    ks = jax.random.split(key, 16)
    nrm = lambda k, shape, scale: jax.random.normal(k, shape, jnp.float32) * scale
    return {
        "x_prompt": nrm(ks[0], (BATCH, SEQ, D_MODEL), 1.0),
        "x_sample": nrm(ks[1], (DEC_BATCH, DEC_SEQ, D_MODEL), 1.0),
        "cache_a_k": nrm(ks[2], (DEPTH, DEC_BATCH, PAST_LEN, N_HEADS_A, HEAD_DIM), 1.0),
        "cache_a_v": nrm(ks[3], (DEPTH, DEC_BATCH, PAST_LEN, N_HEADS_A, HEAD_DIM), 1.0),
        "cache_idx_k": nrm(ks[4], (DEPTH, DEC_BATCH, PAST_LEN, IDX_DIM), 1.0),
        "cache_b_k": nrm(ks[5], (DEPTH, DEC_BATCH, PAST_LEN, N_HEADS_B, HEAD_DIM), 1.0),
        "cache_b_v": nrm(ks[6], (DEPTH, DEC_BATCH, PAST_LEN, N_HEADS_B, HEAD_DIM), 1.0),
        "norm_g": 1.0 + nrm(ks[7], (DEPTH, D_MODEL), 0.02),
        "w_in": nrm(ks[8], (DEPTH, D_MODEL, D_IN), D_MODEL ** -0.5),
        "q_norm_g": 1.0 + nrm(ks[9], (DEPTH, HEAD_DIM), 0.02),
        "k_norm_g": 1.0 + nrm(ks[10], (DEPTH, HEAD_DIM), 0.02),
        "idx_k_norm_g": 1.0 + nrm(ks[11], (DEPTH, IDX_DIM), 0.02),
        "w_a_out": nrm(ks[12], (DEPTH, WIDTH_A, D_MODEL), WIDTH_A ** -0.5),
        "w_b_out": nrm(ks[13], (DEPTH, WIDTH_B, D_MODEL), WIDTH_B ** -0.5),
        "w_o": nrm(ks[14], (DEPTH, D_MODEL, D_MODEL), D_MODEL ** -0.5),
    }


def reference(x_prompt, x_sample, cache_a_k, cache_a_v, cache_idx_k, cache_b_k, cache_b_v,
              norm_g, w_in, q_norm_g, k_norm_g, idx_k_norm_g, w_a_out, w_b_out, w_o):
    seq = x_prompt.shape[1]
    past_len = cache_a_k.shape[2]
    dec_seq = x_sample.shape[1]
    pos_p = jnp.arange(seq, dtype=jnp.int32)
    pos_s = past_len + jnp.arange(dec_seq, dtype=jnp.int32)
    topk_p = min(TOPK_MAX, seq // 4)
    topk_s = min(TOPK_MAX, (past_len + dec_seq) // 4)
    yp, ys = x_prompt, x_sample
    new_p, new_s = [], []
    for l in range(DEPTH):
        params = (norm_g[l], w_in[l], q_norm_g[l], k_norm_g[l], idx_k_norm_g[l], w_a_out[l], w_b_out[l], w_o[l])
        yp, rows_p = _layer(yp, pos_p, None, topk_p, True, *params)
        past = (cache_a_k[l], cache_a_v[l], cache_idx_k[l], cache_b_k[l], cache_b_v[l])
        ys, rows_s = _layer(ys, pos_s, past, topk_s, False, *params)
        new_p.append(rows_p)
        new_s.append(rows_s)
    stk = lambda rows, i: jnp.stack([r[i] for r in rows], axis=0)
    return (yp, ys,
            stk(new_p, 0), stk(new_p, 1), stk(new_p, 2), stk(new_p, 3), stk(new_p, 4),
            stk(new_s, 0), stk(new_s, 1), stk(new_s, 2), stk(new_s, 3), stk(new_s, 4))
```

```python
import functools

import numpy as np
import jax
import jax.numpy as jnp
from jax import lax
from jax.experimental import pallas as pl
from jax.experimental.pallas import tpu as pltpu

F32 = jnp.float32
BF16 = jnp.bfloat16
I32 = jnp.int32

HEAD_DIM = 64
N_HEADS = 8
WIDTH = N_HEADS * HEAD_DIM
IDX_DIM = 64
CHUNK = 64
TOPK_MAX = 256
ROPE_THETA = 500000.0
EPS = 1e-6
NEG_INF = -1e30

LANES = 128
PAIRS = WIDTH // LANES
VMEM_LIMIT_BYTES = 56 * 1024 * 1024

INT_MIN = -(2 ** 31)
INT_MAX = 2 ** 31 - 1
_NEG_BITS = int(np.array(NEG_INF, np.float32).view(np.int32))
NEG_KEY = _NEG_BITS ^ 0x7FFFFFFF

ROW_BLOCK = 256
DSA_TQ = 128
KEY_BLOCK = 256
SB_TQ = 256

_NT = (((1,), (1,)), ((), ()))


def _params(*sem):
    return pltpu.CompilerParams(dimension_semantics=sem, vmem_limit_bytes=VMEM_LIMIT_BYTES)


_G_QA, _G_KA, _G_VA, _G_UA, _G_QI = (i * WIDTH for i in range(5))
_G_KI = 5 * WIDTH
_G_WI = _G_KI + LANES
_G_QB = _G_WI + LANES
_G_KB, _G_VB, _G_UB = (_G_QB + i * WIDTH for i in range(1, 4))
_G_GA = _G_QB + 4 * WIDTH


def _rope(x, cs, s1, s2):
    outs = []
    for j in range(x.shape[1] // LANES):
        xj = x[:, j * LANES:(j + 1) * LANES]
        outs.append(xj * cs + pltpu.roll(xj, LANES - 8, 1) * s1 + pltpu.roll(xj, 8, 1) * s2)
    return outs[0] if len(outs) == 1 else jnp.concatenate(outs, axis=1)


def _proj_kernel(x_ref, g_ref, w_ref, cs_ref, s1_ref, s2_ref, qg_ref, kg_ref, ig_ref, bd_ref,
                 qa_o, ka_o, kab_o, va_o, vab_o, gta_o, qi_o, ki_o, kib_o, wi_o,
                 qb_o, kb_o, kbb_o, vb_o, vbb_o, gtb_o, sga_o, sgb_o, *, d_model):
    x = x_ref[...]
    ms = jnp.mean(x * x, axis=-1, keepdims=True)
    xn = (x * lax.rsqrt(ms + EPS) * g_ref[...]).astype(BF16)
    cs, s1, s2 = cs_ref[...], s1_ref[...], s2_ref[...]

    def proj(c0, width):
        return jnp.dot(xn, w_ref[:, c0:c0 + width], preferred_element_type=F32)

    def head_norm(h, gain):
        ss = jnp.dot((h * h).astype(BF16), bd_ref[...], preferred_element_type=F32)
        return h * lax.rsqrt(ss * (1.0 / HEAD_DIM) + EPS) * gain

    qa = _rope(head_norm(proj(_G_QA, WIDTH), qg_ref[...]), cs, s1, s2)
    qa_o[...] = (qa * (HEAD_DIM ** -0.5)).astype(BF16)
    ka = _rope(head_norm(proj(_G_KA, WIDTH), kg_ref[...]), cs, s1, s2)
    ka_o[...] = ka
    kab_o[...] = ka.astype(BF16)
    va = proj(_G_VA, WIDTH)
    va_o[...] = va
    vab_o[...] = va.astype(BF16)
    ua = proj(_G_UA, WIDTH)
    gta_o[...] = (ua * jax.nn.sigmoid(ua)).astype(BF16)
    qi_o[...] = _rope(proj(_G_QI, WIDTH), cs, s1, s2).astype(BF16)

    kw = proj(_G_KI, 2 * LANES)
    hk = kw[:, :LANES]
    ssk = jnp.sum(hk * hk, axis=-1, keepdims=True)
    kin = hk * lax.rsqrt(ssk * (1.0 / IDX_DIM) + EPS) * ig_ref[...]
    ki = _rope(kin, cs, s1, s2)[:, :IDX_DIM]
    ki_o[...] = ki
    kib_o[...] = ki.astype(BF16)
    wi_o[...] = kw[:, LANES:LANES + N_HEADS] * (N_HEADS ** -0.5) * (IDX_DIM ** -0.5)

    qb_o[...] = (proj(_G_QB, WIDTH) * (HEAD_DIM ** -0.5)).astype(BF16)
    kb = proj(_G_KB, WIDTH)
    kb_o[...] = kb
    kbb_o[...] = kb.astype(BF16)
    vb = proj(_G_VB, WIDTH)
    vb_o[...] = vb
    vbb_o[...] = vb.astype(BF16)
    ub = proj(_G_UB, WIDTH)
    gtb_o[...] = (ub * jax.nn.sigmoid(ub)).astype(BF16)
    sga_o[...] = jax.nn.sigmoid(proj(_G_GA, d_model)).astype(BF16)
    sgb_o[...] = jax.nn.sigmoid(proj(_G_GA + d_model, d_model)).astype(BF16)


def _pack_w_in(w_in, d_model):
    o = 0
    parts = []

    def take(n, pad_to=None):
        nonlocal o
        blk = w_in[:, o:o + n]
        o += n
        if pad_to is not None and pad_to > n:
            blk = jnp.pad(blk, ((0, 0), (0, pad_to - n)))
        parts.append(blk)

    for _ in range(5):
        take(WIDTH)
    take(IDX_DIM, LANES)
    take(N_HEADS, LANES)
    for _ in range(4):
        take(WIDTH)
    take(d_model)
    take(d_model)
    assert o == w_in.shape[1]
    return jnp.concatenate(parts, axis=1).astype(BF16)


def _rope_tables(pos):
    half = HEAD_DIM // 8
    inv_freq = ROPE_THETA ** (-jnp.arange(half, dtype=F32) / half)
    ang = pos.astype(F32)[:, None] * inv_freq[None, :]
    cos, sin = jnp.cos(ang), jnp.sin(ang)
    t = pos.shape[0]
    ones = jnp.ones((t, HEAD_DIM - 2 * half), F32)
    zeros = jnp.zeros((t, HEAD_DIM - 2 * half), F32)
    z8 = jnp.zeros((t, half), F32)
    cs = jnp.concatenate([cos, cos, ones], axis=1)
    s1 = jnp.concatenate([-sin, z8, zeros], axis=1)
    s2 = jnp.concatenate([z8, sin, zeros], axis=1)
    rep = lambda a: jnp.concatenate([a, a], axis=1)
    return rep(cs), rep(s1), rep(s2)


def _project(x2d, pos, t_len, norm_g, w_pack, q_norm_g, k_norm_g, idx_k_norm_g):
    n, d_model = x2d.shape
    rb = min(ROW_BLOCK, n)
    assert n % rb == 0
    cs, s1, s2 = _rope_tables(pos)
    if t_len >= rb:
        assert t_len % rb == 0
        tb = t_len // rb
        tab_map = lambda i: (i % tb, 0)
    else:
        assert rb % t_len == 0
        cs, s1, s2 = (jnp.tile(a, (rb // t_len, 1)) for a in (cs, s1, s2))
        tab_map = lambda i: (0, 0)
    qg = jnp.tile(q_norm_g, N_HEADS)[None, :]
    kg = jnp.tile(k_norm_g, N_HEADS)[None, :]
    ig = jnp.pad(idx_k_norm_g, (0, LANES - IDX_DIM))[None, :]
    hid = np.arange(WIDTH) // HEAD_DIM
    bd = jnp.asarray(hid[:, None] == hid[None, :], BF16)
    e_pack = w_pack.shape[1]

    row = lambda w: pl.BlockSpec((rb, w), lambda i: (i, 0))
    const = lambda shape: pl.BlockSpec(shape, lambda i: (0, 0))
    f32o = lambda w: jax.ShapeDtypeStruct((n, w), F32)
    b16o = lambda w: jax.ShapeDtypeStruct((n, w), BF16)
    outs = [
        (b16o(WIDTH), row(WIDTH)),
        (f32o(WIDTH), row(WIDTH)),
        (b16o(WIDTH), row(WIDTH)),
        (f32o(WIDTH), row(WIDTH)),
        (b16o(WIDTH), row(WIDTH)),
        (b16o(WIDTH), row(WIDTH)),
        (b16o(WIDTH), row(WIDTH)),
        (f32o(IDX_DIM), row(IDX_DIM)),
        (b16o(IDX_DIM), row(IDX_DIM)),
        (f32o(N_HEADS), row(N_HEADS)),
        (b16o(WIDTH), row(WIDTH)),
        (f32o(WIDTH), row(WIDTH)),
        (b16o(WIDTH), row(WIDTH)),
        (f32o(WIDTH), row(WIDTH)),
        (b16o(WIDTH), row(WIDTH)),
        (b16o(WIDTH), row(WIDTH)),
        (b16o(d_model), row(d_model)),
        (b16o(d_model), row(d_model)),
    ]
    return pl.pallas_call(
        functools.partial(_proj_kernel, d_model=d_model),
        grid=(n // rb,),
        in_specs=[row(d_model), const((1, d_model)), const((d_model, e_pack)),
                  pl.BlockSpec((rb, LANES), tab_map), pl.BlockSpec((rb, LANES), tab_map),
                  pl.BlockSpec((rb, LANES), tab_map),
                  const((1, WIDTH)), const((1, WIDTH)), const((1, LANES)), const((WIDTH, WIDTH))],
        out_specs=[o[1] for o in outs],
        out_shape=[o[0] for o in outs],
        compiler_params=_params("parallel"),
    )(x2d, norm_g[None, :], w_pack, cs, s1, s2, qg, kg, ig, bd)


def _sort_key(score):
    b = lax.bitcast_convert_type(score, I32)
    b = jnp.where(b == INT_MIN, 0, b)
    return jnp.where(b < 0, b ^ INT_MAX, b)


def _lane_fold(m):
    part = m[:, :LANES]
    for c in range(1, m.shape[1] // LANES):
        part = part + m[:, c * LANES:(c + 1) * LANES]
    return part


def _chunk_limit(pos):
    shift = CHUNK.bit_length() - 1
    assert CHUNK == 1 << shift
    return ((pos >> shift) + 1) << shift


def _head_masks():
    lane = lax.broadcasted_iota(I32, (1, LANES), 1)
    return lane < HEAD_DIM


def _block_scores(q_heads, wi, ki_blk):
    sc = None
    for h in range(N_HEADS):
        d = lax.dot_general(q_heads[h], ki_blk, _NT, preferred_element_type=F32)
        t = jnp.maximum(d, 0.0) * wi[:, h:h + 1]
        sc = t if sc is None else sc + t
    return sc


def _select_bias(keys_scr, bias_scr, nkb, limit, topk, n_unprocessed, kblk):
    tq = keys_scr.shape[1]

    def count(pred):
        def body(kb, acc):
            return acc + _lane_fold(jnp.where(pred(keys_scr[kb], kb), 1, 0))
        acc = lax.fori_loop(0, nkb, body, jnp.zeros((tq, LANES), I32))
        return jnp.sum(acc, axis=1, keepdims=True)

    def count_ge(cand):
        c = count(lambda k, kb: k >= cand)
        return c + jnp.where(cand <= NEG_KEY, n_unprocessed, 0)

    def search(i, thr):
        cand = thr + lax.shift_left(jnp.int32(1), 31 - i)
        return jnp.where(count_ge(cand) >= topk, cand, thr)

    thr = lax.fori_loop(0, 32, search, jnp.full((tq, 1), INT_MIN, I32))
    c_gt = count_ge(thr + 1)
    c_ge = count_ge(thr)
    need = topk - c_gt
    excess = jnp.max(jnp.where(c_ge - c_gt > need, 1, 0)) > 0

    def key_index(kb):
        return kb * kblk + lax.broadcasted_iota(I32, (tq, kblk), 1)

    nbits = int(keys_scr.shape[0] * kblk).bit_length() + 1

    def tie_cut():
        def step(i, cut):
            cand = cut + lax.shift_left(jnp.int32(1), nbits - 1 - i)
            c = count(lambda k, kb: (k == thr) & (key_index(kb) < cand))
            return jnp.where(c <= need, cand, cut)
        return lax.fori_loop(0, nbits, step, jnp.zeros((tq, 1), I32))

    cut = lax.cond(excess, tie_cut, lambda: jnp.full((tq, 1), INT_MAX, I32))

    def write(kb, _):
        k = keys_scr[kb]
        idx = key_index(kb)
        sel = ((k > thr) | ((k == thr) & (idx < cut))) & (idx < limit)
        bias_scr[kb] = jnp.where(sel, 0.0, NEG_INF)
        return 0

    lax.fori_loop(0, nkb, write, 0)


def _attn_head(qz, load_kv, bias_scr, nkb):
    tq = qz.shape[0]

    def body(kb, carry):
        m, l, acc = carry
        k_blk, v_blk = load_kv(kb)
        lg = lax.dot_general(qz, k_blk, _NT, preferred_element_type=F32) + bias_scr[kb]
        m_new = jnp.maximum(m, jnp.max(lg, axis=1, keepdims=True))
        alpha = jnp.exp(m - m_new)
        p = jnp.exp(lg - m_new)
        l = alpha * l + jnp.sum(p, axis=1, keepdims=True)
        acc = alpha * acc + jnp.dot(p.astype(BF16), v_blk, preferred_element_type=F32)
        return m_new, l, acc

    init = (jnp.full((tq, 1), NEG_INF, F32), jnp.zeros((tq, 1), F32), jnp.zeros((tq, LANES), F32))
    _, l, acc = lax.fori_loop(0, nkb, body, init)
    return acc / l


def _attend_all_heads(qa, make_load_kv, bias_scr, nkb):
    lo = _head_masks()
    outs = []
    for p in range(PAIRS):
        qp = qa[:, p * LANES:(p + 1) * LANES]
        o0 = _attn_head(jnp.where(lo, qp, jnp.zeros_like(qp)), make_load_kv(p), bias_scr, nkb)
        o1 = _attn_head(jnp.where(lo, jnp.zeros_like(qp), qp), make_load_kv(p), bias_scr, nkb)
        outs.append(jnp.where(lo, o0, o1))
    return jnp.concatenate(outs, axis=1)


def _dsa_prompt_kernel(qa_ref, qi_ref, wi_ref, gt_ref, ka_ref, va_ref, ki_ref, o_ref,
                       keys_scr, bias_scr, *, tq, kblk, topk, seq):
    j = pl.program_id(1)
    nkb = ((j + 1) * tq + kblk - 1) // kblk
    pos = j * tq + lax.broadcasted_iota(I32, (tq, 1), 0)
    limit = _chunk_limit(pos)
    qi = qi_ref[0]
    wi = wi_ref[0]
    q_heads = [qi[:, h * IDX_DIM:(h + 1) * IDX_DIM] for h in range(N_HEADS)]

    def score_body(kb, _):
        start = pl.multiple_of(kb * kblk, kblk)
        sc = _block_scores(q_heads, wi, ki_ref[0, pl.ds(start, kblk), :])
        kpos = kb * kblk + lax.broadcasted_iota(I32, (tq, kblk), 1)
        keys_scr[kb] = _sort_key(jnp.where(kpos < limit, sc, NEG_INF))
        return 0

    lax.fori_loop(0, nkb, score_body, 0)
    _select_bias(keys_scr, bias_scr, nkb, limit, topk, seq - nkb * kblk, kblk)

    def make_load_kv(p):
        def load(kb):
            start = pl.multiple_of(kb * kblk, kblk)
            return (ka_ref[0, pl.ds(start, kblk), p * LANES:(p + 1) * LANES],
                    va_ref[0, pl.ds(start, kblk), p * LANES:(p + 1) * LANES])
        return load

    oa = _attend_all_heads(qa_ref[0], make_load_kv, bias_scr, nkb)
    o_ref[0] = (oa * gt_ref[0].astype(F32)).astype(BF16)


def _dsa_prompt(qa, qi, wi, gta, kab, vab, kib, batch, seq):
    tq, kblk = min(DSA_TQ, seq), min(KEY_BLOCK, seq)
    assert seq % tq == 0 and seq % kblk == 0 and tq % CHUNK == 0
    topk = min(TOPK_MAX, seq // 4)
    r3 = lambda a: a.reshape(batch, seq, a.shape[-1])
    qblk = lambda w: pl.BlockSpec((1, tq, w), lambda b, j: (b, j, 0))
    full = lambda w: pl.BlockSpec((1, seq, w), lambda b, j: (b, 0, 0))
    out = pl.pallas_call(
        functools.partial(_dsa_prompt_kernel, tq=tq, kblk=kblk, topk=topk, seq=seq),
        grid=(batch, seq // tq),
        in_specs=[qblk(WIDTH), qblk(WIDTH), qblk(N_HEADS), qblk(WIDTH),
                  full(WIDTH), full(WIDTH), full(IDX_DIM)],
        out_specs=qblk(WIDTH),
        out_shape=jax.ShapeDtypeStruct((batch, seq, WIDTH), BF16),
        scratch_shapes=[pltpu.VMEM((seq // kblk, tq, kblk), I32),
                        pltpu.VMEM((seq // kblk, tq, kblk), F32)],
        compiler_params=_params("parallel", "arbitrary"),
    )(r3(qa), r3(qi), r3(wi), r3(gta), r3(kab), r3(vab), r3(kib))
    return out.reshape(batch * seq, WIDTH)


def _pad_rows(x, rows):
    return jnp.concatenate([x, jnp.zeros((rows - x.shape[0], x.shape[1]), x.dtype)], axis=0)


def _dsa_sample_kernel(qa_ref, qi_ref, wi_ref, gt_ref, ka_ref, va_ref, ki_ref,
                       cka_ref, cva_ref, cki_ref, o_ref, keys_scr, bias_scr,
                       *, tq, kblk, topk, past):
    ncache = past // kblk
    total = past + tq
    pos = past + lax.broadcasted_iota(I32, (tq, 1), 0)
    limit = jnp.minimum(_chunk_limit(pos), total)
    qi = qi_ref[0]
    wi = wi_ref[0]
    q_heads = [qi[:, h * IDX_DIM:(h + 1) * IDX_DIM] for h in range(N_HEADS)]

    def to_keys(sc, kb):
        kpos = kb * kblk + lax.broadcasted_iota(I32, (tq, kblk), 1)
        keys = _sort_key(jnp.where(kpos < limit, sc, NEG_INF))
        return jnp.where(kpos < total, keys, INT_MIN)

    def cache_body(kb, _):
        start = pl.multiple_of(kb * kblk, kblk)
        ki_blk = cki_ref[0, pl.ds(start, kblk), :].astype(BF16)
        keys_scr[kb] = to_keys(_block_scores(q_heads, wi, ki_blk), kb)
        return 0

    lax.fori_loop(0, ncache, cache_body, 0)
    keys_scr[ncache] = to_keys(_block_scores(q_heads, wi, _pad_rows(ki_ref[0], kblk)), ncache)
    nkb = ncache + 1
    _select_bias(keys_scr, bias_scr, nkb, limit, topk, 0, kblk)

    def make_load_kv(p):
        cols = slice(p * LANES, (p + 1) * LANES)
        k_new = _pad_rows(ka_ref[0, :, cols], kblk)
        v_new = _pad_rows(va_ref[0, :, cols], kblk)

        def load(kb):
            start = pl.multiple_of(jnp.minimum(kb, ncache - 1) * kblk, kblk)
            k_c = cka_ref[0, pl.ds(start, kblk), cols].astype(BF16)
            v_c = cva_ref[0, pl.ds(start, kblk), cols].astype(BF16)
            is_new = kb == ncache
            return jnp.where(is_new, k_new, k_c), jnp.where(is_new, v_new, v_c)
        return load

    oa = _attend_all_heads(qa_ref[0], make_load_kv, bias_scr, nkb)
    o_ref[0] = (oa * gt_ref[0].astype(F32)).astype(BF16)


def _dsa_sample(qa, qi, wi, gta, kab, vab, kib, cache_k, cache_v, cache_ki, batch, tq, past):
    kblk = min(KEY_BLOCK, past)
    assert past % kblk == 0 and tq <= kblk
    topk = min(TOPK_MAX, (past + tq) // 4)
    assert past + tq >= topk
    r3 = lambda a: a.reshape(batch, tq, a.shape[-1])
    new = lambda w: pl.BlockSpec((1, tq, w), lambda b: (b, 0, 0))
    cache = lambda w: pl.BlockSpec((1, past, w), lambda b: (b, 0, 0))
    nkb = past // kblk + 1
    out = pl.pallas_call(
        functools.partial(_dsa_sample_kernel, tq=tq, kblk=kblk, topk=topk, past=past),
        grid=(batch,),
        in_specs=[new(WIDTH), new(WIDTH), new(N_HEADS), new(WIDTH),
                  new(WIDTH), new(WIDTH), new(IDX_DIM),
                  cache(WIDTH), cache(WIDTH), cache(IDX_DIM)],
        out_specs=new(WIDTH),
        out_shape=jax.ShapeDtypeStruct((batch, tq, WIDTH), BF16),
        scratch_shapes=[pltpu.VMEM((nkb, tq, kblk), I32), pltpu.VMEM((nkb, tq, kblk), F32)],
        compiler_params=_params("parallel"),
    )(r3(qa), r3(qi), r3(wi), r3(gta), r3(kab), r3(vab), r3(kib),
      cache_k.reshape(batch, past, WIDTH), cache_v.reshape(batch, past, WIDTH), cache_ki)
    return out.reshape(batch * tq, WIDTH)


def _sb_block(qz, k_blk, v_blk, tri, carry, acc, mask):
    z = lax.dot_general(qz, k_blk, _NT, preferred_element_type=F32)
    lg = jnp.log(1.0 + jnp.exp(-jnp.abs(z)))
    log1m = -(jnp.maximum(z, 0.0) + lg)
    if mask is not None:
        log1m = jnp.where(mask, log1m, 0.0)
    logsig = jnp.minimum(z, 0.0) - lg
    hi = log1m.astype(BF16)
    lo = (log1m - hi.astype(F32)).astype(BF16)
    after = jnp.dot(jnp.concatenate([hi, lo], axis=1), tri, preferred_element_type=F32)
    a = jnp.exp(logsig + after + carry)
    if mask is not None:
        a = jnp.where(mask, a, 0.0)
    acc = acc + jnp.dot(a.astype(BF16), v_blk, preferred_element_type=F32)
    carry = carry + jnp.sum(log1m, axis=1, keepdims=True)
    return carry, acc


def _sb_tri(kblk):
    jj = np.arange(kblk)
    m = (jj[:, None] > jj[None, :])
    return jnp.asarray(np.concatenate([m, m], axis=0), BF16)


def _sb_prompt_kernel(qb_ref, gt_ref, kb_ref, vb_ref, tri_ref, o_ref, *, tq):
    j = pl.program_id(1)
    qb = qb_ref[0]
    tri = tri_ref[...]
    lo = _head_masks()
    diag = (lax.broadcasted_iota(I32, (tq, tq), 1) < lax.broadcasted_iota(I32, (tq, tq), 0))
    outs = []
    for p in range(PAIRS):
        cols = slice(p * LANES, (p + 1) * LANES)
        qp = qb[:, cols]
        o_pair = []
        for s in range(2):
            qz = jnp.where(lo, qp, jnp.zeros_like(qp)) if s == 0 else jnp.where(lo, jnp.zeros_like(qp), qp)

            def load(kb):
                start = pl.multiple_of(kb * tq, tq)
                return kb_ref[0, pl.ds(start, tq), cols], vb_ref[0, pl.ds(start, tq), cols]

            k_d, v_d = load(j)
            carry, acc = _sb_block(qz, k_d, v_d, tri, jnp.zeros((tq, 1), F32),
                                   jnp.zeros((tq, LANES), F32), diag)

            def body(i, st):
                k_b, v_b = load(j - 1 - i)
                return _sb_block(qz, k_b, v_b, tri, st[0], st[1], None)

            _, acc = lax.fori_loop(0, j, body, (carry, acc))
            o_pair.append(acc)
        outs.append(jnp.where(lo, o_pair[0], o_pair[1]))
    ob = jnp.concatenate(outs, axis=1)
    o_ref[0] = (ob * gt_ref[0].astype(F32)).astype(BF16)


def _sb_prompt(qb, gtb, kbb, vbb, batch, seq):
    tq = min(SB_TQ, seq)
    assert seq % tq == 0
    r3 = lambda a: a.reshape(batch, seq, a.shape[-1])
    qblk = pl.BlockSpec((1, tq, WIDTH), lambda b, j: (b, j, 0))
    full = pl.BlockSpec((1, seq, WIDTH), lambda b, j: (b, 0, 0))
    out = pl.pallas_call(
        functools.partial(_sb_prompt_kernel, tq=tq),
        grid=(batch, seq // tq),
        in_specs=[qblk, qblk, full, full, pl.BlockSpec((2 * tq, tq), lambda b, j: (0, 0))],
        out_specs=qblk,
        out_shape=jax.ShapeDtypeStruct((batch, seq, WIDTH), BF16),
        compiler_params=_params("parallel", "arbitrary"),
    )(r3(qb), r3(gtb), r3(kbb), r3(vbb), _sb_tri(tq))
    return out.reshape(batch * seq, WIDTH)


def _sb_sample_kernel(qb_ref, gt_ref, kb_ref, vb_ref, ckb_ref, cvb_ref, tri_ref, o_ref,
                      *, tq, kblk, past):
    ncache = past // kblk
    qb = qb_ref[0]
    tri = tri_ref[...]
    lo = _head_masks()
    diag = (lax.broadcasted_iota(I32, (tq, kblk), 1) < lax.broadcasted_iota(I32, (tq, kblk), 0))
    outs = []
    for p in range(PAIRS):
        cols = slice(p * LANES, (p + 1) * LANES)
        qp = qb[:, cols]
        o_pair = []
        for s in range(2):
            qz = jnp.where(lo, qp, jnp.zeros_like(qp)) if s == 0 else jnp.where(lo, jnp.zeros_like(qp), qp)
            k_new = _pad_rows(kb_ref[0, :, cols], kblk)
            v_new = _pad_rows(vb_ref[0, :, cols], kblk)
            carry, acc = _sb_block(qz, k_new, v_new, tri, jnp.zeros((tq, 1), F32),
                                   jnp.zeros((tq, LANES), F32), diag)

            def body(i, st):
                start = pl.multiple_of((ncache - 1 - i) * kblk, kblk)
                k_b = ckb_ref[0, pl.ds(start, kblk), cols].astype(BF16)
                v_b = cvb_ref[0, pl.ds(start, kblk), cols].astype(BF16)
                return _sb_block(qz, k_b, v_b, tri, st[0], st[1], None)

            _, acc = lax.fori_loop(0, ncache, body, (carry, acc))
            o_pair.append(acc)
        outs.append(jnp.where(lo, o_pair[0], o_pair[1]))
    ob = jnp.concatenate(outs, axis=1)
    o_ref[0] = (ob * gt_ref[0].astype(F32)).astype(BF16)


def _sb_sample(qb, gtb, kbb, vbb, cache_k, cache_v, batch, tq, past):
    kblk = min(KEY_BLOCK, past)
    assert past % kblk == 0 and tq <= kblk
    r3 = lambda a: a.reshape(batch, tq, a.shape[-1])
    new = pl.BlockSpec((1, tq, WIDTH), lambda b: (b, 0, 0))
    cache = pl.BlockSpec((1, past, WIDTH), lambda b: (b, 0, 0))
    out = pl.pallas_call(
        functools.partial(_sb_sample_kernel, tq=tq, kblk=kblk, past=past),
        grid=(batch,),
        in_specs=[new, new, new, new, cache, cache, pl.BlockSpec((2 * kblk, kblk), lambda b: (0, 0))],
        out_specs=new,
        out_shape=jax.ShapeDtypeStruct((batch, tq, WIDTH), BF16),
        compiler_params=_params("parallel"),
    )(r3(qb), r3(gtb), r3(kbb), r3(vbb), cache_k.reshape(batch, past, WIDTH),
      cache_v.reshape(batch, past, WIDTH), _sb_tri(kblk))
    return out.reshape(batch * tq, WIDTH)


def _merge_kernel(x_ref, ta_ref, tb_ref, sga_ref, sgb_ref, wa_ref, wb_ref, wo_ref, y_ref):
    ya = jnp.dot(ta_ref[...], wa_ref[...], preferred_element_type=F32)
    yb = jnp.dot(tb_ref[...], wb_ref[...], preferred_element_type=F32)
    mixed = sga_ref[...].astype(F32) * ya + sgb_ref[...].astype(F32) * yb
    y_ref[...] = x_ref[...] + jnp.dot(mixed.astype(BF16), wo_ref[...], preferred_element_type=F32)


def _merge(x2d, ta, tb, sga, sgb, wa, wb, wo):
    n, d_model = x2d.shape
    rb = min(ROW_BLOCK, n)
    row = lambda w: pl.BlockSpec((rb, w), lambda i: (i, 0))
    const = lambda a: pl.BlockSpec(a.shape, lambda i: (0, 0))
    return pl.pallas_call(
        _merge_kernel,
        grid=(n // rb,),
        in_specs=[row(d_model), row(WIDTH), row(WIDTH), row(d_model), row(d_model),
                  const(wa), const(wb), const(wo)],
        out_specs=row(d_model),
        out_shape=jax.ShapeDtypeStruct((n, d_model), F32),
        compiler_params=_params("parallel"),
    )(x2d, ta, tb, sga, sgb, wa, wb, wo)


def _layer(x, pos, past, params):
    norm_g, w_pack, q_norm_g, k_norm_g, idx_k_norm_g, wa, wb, wo = params
    b, t, d_model = x.shape
    x2d = x.reshape(b * t, d_model)
    (qa, ka, kab, va, vab, gta, qi, ki, kib, wi, qb, kb, kbb, vb, vbb, gtb, sga, sgb) = _project(
        x2d, pos, t, norm_g, w_pack, q_norm_g, k_norm_g, idx_k_norm_g)
    if past is None:
        ta = _dsa_prompt(qa, qi, wi, gta, kab, vab, kib, b, t)
        tb = _sb_prompt(qb, gtb, kbb, vbb, b, t)
    else:
        c_ak, c_av, c_ik, c_bk, c_bv = past
        p_len = c_ak.shape[1]
        ta = _dsa_sample(qa, qi, wi, gta, kab, vab, kib, c_ak, c_av, c_ik, b, t, p_len)
        tb = _sb_sample(qb, gtb, kbb, vbb, c_bk, c_bv, b, t, p_len)
    y = _merge(x2d, ta, tb, sga, sgb, wa, wb, wo).reshape(b, t, d_model)
    rows = (ka.reshape(b, t, N_HEADS, HEAD_DIM), va.reshape(b, t, N_HEADS, HEAD_DIM),
            ki.reshape(b, t, IDX_DIM), kb.reshape(b, t, N_HEADS, HEAD_DIM),
            vb.reshape(b, t, N_HEADS, HEAD_DIM))
    return y, rows


def kernel(x_prompt, x_sample, cache_a_k, cache_a_v, cache_idx_k, cache_b_k, cache_b_v,
           norm_g, w_in, q_norm_g, k_norm_g, idx_k_norm_g, w_a_out, w_b_out, w_o):
    depth = norm_g.shape[0]
    d_model = x_prompt.shape[2]
    seq = x_prompt.shape[1]
    past_len = cache_a_k.shape[2]
    dec_seq = x_sample.shape[1]
    pos_p = jnp.arange(seq, dtype=I32)
    pos_s = past_len + jnp.arange(dec_seq, dtype=I32)
    yp, ys = x_prompt, x_sample
    new_p, new_s = [], []
    for l in range(depth):
        params = (norm_g[l], _pack_w_in(w_in[l], d_model), q_norm_g[l], k_norm_g[l], idx_k_norm_g[l],
                  w_a_out[l].astype(BF16), w_b_out[l].astype(BF16), w_o[l].astype(BF16))
        yp, rows_p = _layer(yp, pos_p, None, params)
        past = (cache_a_k[l], cache_a_v[l], cache_idx_k[l], cache_b_k[l], cache_b_v[l])
        ys, rows_s = _layer(ys, pos_s, past, params)
        new_p.append(rows_p)
        new_s.append(rows_s)
    stk = lambda rows, i: jnp.stack([r[i] for r in rows], axis=0)
    return (yp, ys,
            stk(new_p, 0), stk(new_p, 1), stk(new_p, 2), stk(new_p, 3), stk(new_p, 4),
            stk(new_s, 0), stk(new_s, 1), stk(new_s, 2), stk(new_s, 3), stk(new_s, 4))
```

```python
import functools

import numpy as np
import jax
import jax.numpy as jnp
from jax import lax
from jax.experimental import pallas as pl
from jax.experimental.pallas import tpu as pltpu

F32 = jnp.float32
BF16 = jnp.bfloat16
I32 = jnp.int32

HEAD_DIM = 64
N_HEADS = 8
WIDTH = N_HEADS * HEAD_DIM
IDX_DIM = 64
CHUNK = 64
TOPK_MAX = 256
ROPE_THETA = 500000.0
EPS = 1e-6
NEG_INF = -1e30

LANES = 128
PAIRS = WIDTH // LANES
VMEM_LIMIT_BYTES = 56 * 1024 * 1024

INT_MIN = -(2 ** 31)
INT_MAX = 2 ** 31 - 1
_NEG_BITS = int(np.array(NEG_INF, np.float32).view(np.int32))
NEG_KEY = _NEG_BITS ^ 0x7FFFFFFF

ROW_BLOCK = 256
DSA_TQ = 128
KEY_BLOCK = 256
SB_TQ = 256

_NT = (((1,), (1,)), ((), ()))


def _params(*sem):
    return pltpu.CompilerParams(dimension_semantics=sem, vmem_limit_bytes=VMEM_LIMIT_BYTES)


_G_QA, _G_KA, _G_VA, _G_UA, _G_QI = (i * WIDTH for i in range(5))
_G_KI = 5 * WIDTH
_G_WI = _G_KI + LANES
_G_QB = _G_WI + LANES
_G_KB, _G_VB, _G_UB = (_G_QB + i * WIDTH for i in range(1, 4))
_G_GA = _G_QB + 4 * WIDTH


def _rope(x, cs, s1, s2):
    outs = []
    for j in range(x.shape[1] // LANES):
        xj = x[:, j * LANES:(j + 1) * LANES]
        outs.append(xj * cs + pltpu.roll(xj, LANES - 8, 1) * s1 + pltpu.roll(xj, 8, 1) * s2)
    return outs[0] if len(outs) == 1 else jnp.concatenate(outs, axis=1)


def _proj_kernel(x_ref, g_ref, w_ref, cs_ref, s1_ref, s2_ref, qg_ref, kg_ref, ig_ref, bd_ref,
                 qa_o, ka_o, kab_o, va_o, vab_o, gta_o, qi_o, ki_o, kib_o, wi_o,
                 qb_o, kb_o, kbb_o, vb_o, vbb_o, gtb_o, sga_o, sgb_o, *, d_model):
    x = x_ref[...]
    ms = jnp.mean(x * x, axis=-1, keepdims=True)
    xn = (x * lax.rsqrt(ms + EPS) * g_ref[...]).astype(BF16)
    cs, s1, s2 = cs_ref[...], s1_ref[...], s2_ref[...]

    def proj(c0, width):
        return jnp.dot(xn, w_ref[:, c0:c0 + width], preferred_element_type=F32)

    def head_norm(h, gain):
        ss = jnp.dot((h * h).astype(BF16), bd_ref[...], preferred_element_type=F32)
        return h * lax.rsqrt(ss * (1.0 / HEAD_DIM) + EPS) * gain

    qa = _rope(head_norm(proj(_G_QA, WIDTH), qg_ref[...]), cs, s1, s2)
    qa_o[...] = (qa * (HEAD_DIM ** -0.5)).astype(BF16)
    ka = _rope(head_norm(proj(_G_KA, WIDTH), kg_ref[...]), cs, s1, s2)
    ka_o[...] = ka
    kab_o[...] = ka.astype(BF16)
    va = proj(_G_VA, WIDTH)
    va_o[...] = va
    vab_o[...] = va.astype(BF16)
    ua = proj(_G_UA, WIDTH)
    gta_o[...] = (ua * jax.nn.sigmoid(ua)).astype(BF16)
    qi_o[...] = _rope(proj(_G_QI, WIDTH), cs, s1, s2).astype(BF16)

    kw = proj(_G_KI, 2 * LANES)
    hk = kw[:, :LANES]
    ssk = jnp.sum(hk * hk, axis=-1, keepdims=True)
    kin = hk * lax.rsqrt(ssk * (1.0 / IDX_DIM) + EPS) * ig_ref[...]
    ki = _rope(kin, cs, s1, s2)[:, :IDX_DIM]
    ki_o[...] = ki
    kib_o[...] = ki.astype(BF16)
    wi_o[...] = kw[:, LANES:LANES + N_HEADS] * (N_HEADS ** -0.5) * (IDX_DIM ** -0.5)

    qb_o[...] = (proj(_G_QB, WIDTH) * (HEAD_DIM ** -0.5)).astype(BF16)
    kb = proj(_G_KB, WIDTH)
    kb_o[...] = kb
    kbb_o[...] = kb.astype(BF16)
    vb = proj(_G_VB, WIDTH)
    vb_o[...] = vb
    vbb_o[...] = vb.astype(BF16)
    ub = proj(_G_UB, WIDTH)
    gtb_o[...] = (ub * jax.nn.sigmoid(ub)).astype(BF16)
    sga_o[...] = jax.nn.sigmoid(proj(_G_GA, d_model)).astype(BF16)
    sgb_o[...] = jax.nn.sigmoid(proj(_G_GA + d_model, d_model)).astype(BF16)


def _pack_w_in(w_in, d_model):
    o = 0
    parts = []

    def take(n, pad_to=None):
        nonlocal o
        blk = w_in[:, o:o + n]
        o += n
        if pad_to is not None and pad_to > n:
            blk = jnp.pad(blk, ((0, 0), (0, pad_to - n)))
        parts.append(blk)

    for _ in range(5):
        take(WIDTH)
    take(IDX_DIM, LANES)
    take(N_HEADS, LANES)
    for _ in range(4):
        take(WIDTH)
    take(d_model)
    take(d_model)
    assert o == w_in.shape[1]
    return jnp.concatenate(parts, axis=1).astype(BF16)


def _rope_tables(pos):
    half = HEAD_DIM // 8
    inv_freq = ROPE_THETA ** (-jnp.arange(half, dtype=F32) / half)
    ang = pos.astype(F32)[:, None] * inv_freq[None, :]
    cos, sin = jnp.cos(ang), jnp.sin(ang)
    t = pos.shape[0]
    ones = jnp.ones((t, HEAD_DIM - 2 * half), F32)
    zeros = jnp.zeros((t, HEAD_DIM - 2 * half), F32)
    z8 = jnp.zeros((t, half), F32)
    cs = jnp.concatenate([cos, cos, ones], axis=1)
    s1 = jnp.concatenate([-sin, z8, zeros], axis=1)
    s2 = jnp.concatenate([z8, sin, zeros], axis=1)
    rep = lambda a: jnp.concatenate([a, a], axis=1)
    return rep(cs), rep(s1), rep(s2)


def _project(x2d, pos, t_len, norm_g, w_pack, q_norm_g, k_norm_g, idx_k_norm_g):
    n, d_model = x2d.shape
    rb = min(ROW_BLOCK, n)
    assert n % rb == 0
    cs, s1, s2 = _rope_tables(pos)
    if t_len >= rb:
        assert t_len % rb == 0
        tb = t_len // rb
        tab_map = lambda i: (i % tb, 0)
    else:
        assert rb % t_len == 0
        cs, s1, s2 = (jnp.tile(a, (rb // t_len, 1)) for a in (cs, s1, s2))
        tab_map = lambda i: (0, 0)
    qg = jnp.tile(q_norm_g, N_HEADS)[None, :]
    kg = jnp.tile(k_norm_g, N_HEADS)[None, :]
    ig = jnp.pad(idx_k_norm_g, (0, LANES - IDX_DIM))[None, :]
    hid = np.arange(WIDTH) // HEAD_DIM
    bd = jnp.asarray(hid[:, None] == hid[None, :], BF16)
    e_pack = w_pack.shape[1]

    row = lambda w: pl.BlockSpec((rb, w), lambda i: (i, 0))
    const = lambda shape: pl.BlockSpec(shape, lambda i: (0, 0))
    f32o = lambda w: jax.ShapeDtypeStruct((n, w), F32)
    b16o = lambda w: jax.ShapeDtypeStruct((n, w), BF16)
    outs = [
        (b16o(WIDTH), row(WIDTH)),
        (f32o(WIDTH), row(WIDTH)),
        (b16o(WIDTH), row(WIDTH)),
        (f32o(WIDTH), row(WIDTH)),
        (b16o(WIDTH), row(WIDTH)),
        (b16o(WIDTH), row(WIDTH)),
        (b16o(WIDTH), row(WIDTH)),
        (f32o(IDX_DIM), row(IDX_DIM)),
        (b16o(IDX_DIM), row(IDX_DIM)),
        (f32o(N_HEADS), row(N_HEADS)),
        (b16o(WIDTH), row(WIDTH)),
        (f32o(WIDTH), row(WIDTH)),
        (b16o(WIDTH), row(WIDTH)),
        (f32o(WIDTH), row(WIDTH)),
        (b16o(WIDTH), row(WIDTH)),
        (b16o(WIDTH), row(WIDTH)),
        (b16o(d_model), row(d_model)),
        (b16o(d_model), row(d_model)),
    ]
    return pl.pallas_call(
        functools.partial(_proj_kernel, d_model=d_model),
        grid=(n // rb,),
        in_specs=[row(d_model), const((1, d_model)), const((d_model, e_pack)),
                  pl.BlockSpec((rb, LANES), tab_map), pl.BlockSpec((rb, LANES), tab_map),
                  pl.BlockSpec((rb, LANES), tab_map),
                  const((1, WIDTH)), const((1, WIDTH)), const((1, LANES)), const((WIDTH, WIDTH))],
        out_specs=[o[1] for o in outs],
        out_shape=[o[0] for o in outs],
        compiler_params=_params("parallel"),
    )(x2d, norm_g[None, :], w_pack, cs, s1, s2, qg, kg, ig, bd)


def _sort_key(score):
    b = lax.bitcast_convert_type(score, I32)
    b = jnp.where(b == INT_MIN, 0, b)
    return jnp.where(b < 0, b ^ INT_MAX, b)


def _lane_tiles(x):
    return [x[:, c * LANES:(c + 1) * LANES] for c in range(x.shape[1] // LANES)]


def _chunk_limit(pos):
    shift = CHUNK.bit_length() - 1
    assert CHUNK == 1 << shift
    return ((pos >> shift) + 1) << shift


def _head_masks():
    lane = lax.broadcasted_iota(I32, (1, LANES), 1)
    return lane < HEAD_DIM


def _masked_heads(q):
    lo = _head_masks()
    out = []
    for qp in _lane_tiles(q):
        zero = jnp.zeros_like(qp)
        out += [jnp.where(lo, qp, zero), jnp.where(lo, zero, qp)]
    return out


def _index_queries(qi, wi):
    tq = qi.shape[0]
    q_heads = [qi[:, h * IDX_DIM:(h + 1) * IDX_DIM] for h in range(N_HEADS)]
    wib = [jnp.broadcast_to(wi[:, h:h + 1], (tq, LANES)) for h in range(N_HEADS)]
    return q_heads, wib


def _block_scores(q_heads, wib, ki_blk):
    tiles = None
    for h in range(N_HEADS):
        d = lax.dot_general(q_heads[h], ki_blk, _NT, preferred_element_type=F32)
        t = [jnp.maximum(dc, 0.0) * wib[h] for dc in _lane_tiles(d)]
        tiles = t if tiles is None else [a + b for a, b in zip(tiles, t)]
    return jnp.concatenate(tiles, axis=1)


def _select_bias(keys_scr, bias_scr, nkb, limit, topk, n_unprocessed, kblk):
    tq = keys_scr.shape[1]

    def count(pred):
        def body(kb, acc):
            hit = jnp.where(pred(keys_scr[kb], kb), 1, 0)
            return acc + functools.reduce(jnp.add, _lane_tiles(hit))
        acc = lax.fori_loop(0, nkb, body, jnp.zeros((tq, LANES), I32))
        return jnp.sum(acc, axis=1, keepdims=True)

    def count_ge(cand):
        c = count(lambda k, kb: k >= cand)
        return c + jnp.where(cand <= NEG_KEY, n_unprocessed, 0)

    def search(i, thr):
        cand = thr + lax.shift_left(jnp.int32(1), 31 - i)
        return jnp.where(count_ge(cand) >= topk, cand, thr)

    thr = lax.fori_loop(0, 32, search, jnp.full((tq, 1), INT_MIN, I32))
    c_gt = count_ge(thr + 1)
    c_ge = count_ge(thr)
    need = topk - c_gt
    excess = jnp.max(jnp.where(c_ge - c_gt > need, 1, 0)) > 0

    def key_index(kb):
        return kb * kblk + lax.broadcasted_iota(I32, (tq, kblk), 1)

    nbits = int(keys_scr.shape[0] * kblk).bit_length() + 1

    def tie_cut():
        def step(i, cut):
            cand = cut + lax.shift_left(jnp.int32(1), nbits - 1 - i)
            c = count(lambda k, kb: (k == thr) & (key_index(kb) < cand))
            return jnp.where(c <= need, cand, cut)
        return lax.fori_loop(0, nbits, step, jnp.zeros((tq, 1), I32))

    cut = lax.cond(excess, tie_cut, lambda: jnp.full((tq, 1), INT_MAX, I32))

    def write(kb, c):
        k = keys_scr[kb]
        idx = key_index(kb)
        sel = ((k > thr) | ((k == thr) & (idx < cut))) & (idx < limit)
        bias_scr[kb] = jnp.where(sel, 0.0, NEG_INF)
        return c

    lax.fori_loop(0, nkb, write, 0)


def _attention(qa, loop_n, load_k, load_v, tail, bias_scr, lg_scr, mx_scr, l_scr, acc_scr):
    tq = qa.shape[0]
    qz = _masked_heads(qa)
    mx_scr[...] = jnp.full(mx_scr.shape, NEG_INF, F32)

    def logits(kb, k_pairs):
        bias = bias_scr[kb]
        for h in range(N_HEADS):
            lg = lax.dot_general(qz[h], k_pairs[h // 2], _NT, preferred_element_type=F32) + bias
            lg_scr[h, kb] = lg
            mx_scr[h] = jnp.maximum(mx_scr[h], functools.reduce(jnp.maximum, _lane_tiles(lg)))

    def pass1(kb, c):
        logits(kb, load_k(kb))
        return c

    lax.fori_loop(0, loop_n, pass1, 0)
    if tail is not None:
        logits(tail[0], tail[1])
    for h in range(N_HEADS):
        mx_scr[h] = jnp.broadcast_to(jnp.max(mx_scr[h], axis=1, keepdims=True), (tq, LANES))
    l_scr[...] = jnp.zeros(l_scr.shape, F32)
    acc_scr[...] = jnp.zeros(acc_scr.shape, F32)

    def weigh(kb, v_pairs):
        for h in range(N_HEADS):
            m = mx_scr[h]
            p_tiles = [jnp.exp(t - m) for t in _lane_tiles(lg_scr[h, kb])]
            l_scr[h] += functools.reduce(jnp.add, p_tiles)
            p = jnp.concatenate(p_tiles, axis=1).astype(BF16)
            acc_scr[h] += jnp.dot(p, v_pairs[h // 2], preferred_element_type=F32)

    def pass2(kb, c):
        weigh(kb, load_v(kb))
        return c

    lax.fori_loop(0, loop_n, pass2, 0)
    if tail is not None:
        weigh(tail[0], tail[2])
    lo = _head_masks()
    outs = []
    for p in range(PAIRS):
        o = [acc_scr[2 * p + s] / jnp.sum(l_scr[2 * p + s], axis=1, keepdims=True) for s in range(2)]
        outs.append(jnp.where(lo, o[0], o[1]))
    return jnp.concatenate(outs, axis=1)


def _attention_scratch(nkb, tq, kblk):
    return [pltpu.VMEM((nkb, tq, kblk), I32),
            pltpu.VMEM((nkb, tq, kblk), F32),
            pltpu.VMEM((N_HEADS, nkb, tq, kblk), F32),
            pltpu.VMEM((N_HEADS, tq, LANES), F32),
            pltpu.VMEM((N_HEADS, tq, LANES), F32),
            pltpu.VMEM((N_HEADS, tq, LANES), F32)]


def _pair_blocks(ref, start, rows, cast=False):
    out = [ref[0, pl.ds(start, rows), p * LANES:(p + 1) * LANES] for p in range(PAIRS)]
    return [o.astype(BF16) for o in out] if cast else out


def _dsa_prompt_kernel(qa_ref, qi_ref, wi_ref, gt_ref, ka_ref, va_ref, ki_ref, o_ref,
                       keys_scr, bias_scr, lg_scr, mx_scr, l_scr, acc_scr, *, tq, kblk, topk, seq):
    j = pl.program_id(1)
    nkb = ((j + 1) * tq + kblk - 1) // kblk
    pos = j * tq + lax.broadcasted_iota(I32, (tq, 1), 0)
    limit = _chunk_limit(pos)
    q_heads, wib = _index_queries(qi_ref[0], wi_ref[0])

    def score_body(kb, c):
        start = pl.multiple_of(kb * kblk, kblk)
        sc = _block_scores(q_heads, wib, ki_ref[0, pl.ds(start, kblk), :])
        kpos = kb * kblk + lax.broadcasted_iota(I32, (tq, kblk), 1)
        keys_scr[kb] = _sort_key(jnp.where(kpos < limit, sc, NEG_INF))
        return c

    lax.fori_loop(0, nkb, score_body, 0)
    _select_bias(keys_scr, bias_scr, nkb, limit, topk, seq - nkb * kblk, kblk)

    load_k = lambda kb: _pair_blocks(ka_ref, pl.multiple_of(kb * kblk, kblk), kblk)
    load_v = lambda kb: _pair_blocks(va_ref, pl.multiple_of(kb * kblk, kblk), kblk)
    oa = _attention(qa_ref[0], nkb, load_k, load_v, None, bias_scr, lg_scr, mx_scr, l_scr, acc_scr)
    o_ref[0] = (oa * gt_ref[0].astype(F32)).astype(BF16)


def _dsa_prompt(qa, qi, wi, gta, kab, vab, kib, batch, seq):
    tq, kblk = min(DSA_TQ, seq), min(KEY_BLOCK, seq)
    assert seq % tq == 0 and seq % kblk == 0 and tq % CHUNK == 0
    topk = min(TOPK_MAX, seq // 4)
    r3 = lambda a: a.reshape(batch, seq, a.shape[-1])
    qblk = lambda w: pl.BlockSpec((1, tq, w), lambda b, j: (b, j, 0))
    full = lambda w: pl.BlockSpec((1, seq, w), lambda b, j: (b, 0, 0))
    out = pl.pallas_call(
        functools.partial(_dsa_prompt_kernel, tq=tq, kblk=kblk, topk=topk, seq=seq),
        grid=(batch, seq // tq),
        in_specs=[qblk(WIDTH), qblk(WIDTH), qblk(N_HEADS), qblk(WIDTH),
                  full(WIDTH), full(WIDTH), full(IDX_DIM)],
        out_specs=qblk(WIDTH),
        out_shape=jax.ShapeDtypeStruct((batch, seq, WIDTH), BF16),
        scratch_shapes=_attention_scratch(seq // kblk, tq, kblk),
        compiler_params=_params("parallel", "arbitrary"),
    )(r3(qa), r3(qi), r3(wi), r3(gta), r3(kab), r3(vab), r3(kib))
    return out.reshape(batch * seq, WIDTH)


def _pad_rows(x, rows):
    return jnp.concatenate([x, jnp.zeros((rows - x.shape[0], x.shape[1]), x.dtype)], axis=0)


def _dsa_sample_kernel(qa_ref, qi_ref, wi_ref, gt_ref, ka_ref, va_ref, ki_ref,
                       cka_ref, cva_ref, cki_ref, o_ref,
                       keys_scr, bias_scr, lg_scr, mx_scr, l_scr, acc_scr, *, tq, kblk, topk, past):
    ncache = past // kblk
    total = past + tq
    pos = past + lax.broadcasted_iota(I32, (tq, 1), 0)
    limit = jnp.minimum(_chunk_limit(pos), total)
    q_heads, wib = _index_queries(qi_ref[0], wi_ref[0])

    def to_keys(sc, kb):
        kpos = kb * kblk + lax.broadcasted_iota(I32, (tq, kblk), 1)
        keys = _sort_key(jnp.where(kpos < limit, sc, NEG_INF))
        return jnp.where(kpos < total, keys, INT_MIN)

    def cache_body(kb, c):
        start = pl.multiple_of(kb * kblk, kblk)
        ki_blk = cki_ref[0, pl.ds(start, kblk), :].astype(BF16)
        keys_scr[kb] = to_keys(_block_scores(q_heads, wib, ki_blk), kb)
        return c

    lax.fori_loop(0, ncache, cache_body, 0)
    keys_scr[ncache] = to_keys(_block_scores(q_heads, wib, _pad_rows(ki_ref[0], kblk)), ncache)
    _select_bias(keys_scr, bias_scr, ncache + 1, limit, topk, 0, kblk)

    load_k = lambda kb: _pair_blocks(cka_ref, pl.multiple_of(kb * kblk, kblk), kblk, cast=True)
    load_v = lambda kb: _pair_blocks(cva_ref, pl.multiple_of(kb * kblk, kblk), kblk, cast=True)
    tail = (ncache, [_pad_rows(t, kblk) for t in _lane_tiles(ka_ref[0])],
            [_pad_rows(t, kblk) for t in _lane_tiles(va_ref[0])])
    oa = _attention(qa_ref[0], ncache, load_k, load_v, tail, bias_scr, lg_scr, mx_scr, l_scr, acc_scr)
    o_ref[0] = (oa * gt_ref[0].astype(F32)).astype(BF16)


def _dsa_sample(qa, qi, wi, gta, kab, vab, kib, cache_k, cache_v, cache_ki, batch, tq, past):
    kblk = min(KEY_BLOCK, past)
    assert past % kblk == 0 and tq <= kblk
    topk = min(TOPK_MAX, (past + tq) // 4)
    assert past + tq >= topk
    r3 = lambda a: a.reshape(batch, tq, a.shape[-1])
    new = lambda w: pl.BlockSpec((1, tq, w), lambda b: (b, 0, 0))
    cache = lambda w: pl.BlockSpec((1, past, w), lambda b: (b, 0, 0))
    out = pl.pallas_call(
        functools.partial(_dsa_sample_kernel, tq=tq, kblk=kblk, topk=topk, past=past),
        grid=(batch,),
        in_specs=[new(WIDTH), new(WIDTH), new(N_HEADS), new(WIDTH),
                  new(WIDTH), new(WIDTH), new(IDX_DIM),
                  cache(WIDTH), cache(WIDTH), cache(IDX_DIM)],
        out_specs=new(WIDTH),
        out_shape=jax.ShapeDtypeStruct((batch, tq, WIDTH), BF16),
        scratch_shapes=_attention_scratch(past // kblk + 1, tq, kblk),
        compiler_params=_params("parallel"),
    )(r3(qa), r3(qi), r3(wi), r3(gta), r3(kab), r3(vab), r3(kib),
      cache_k.reshape(batch, past, WIDTH), cache_v.reshape(batch, past, WIDTH), cache_ki)
    return out.reshape(batch * tq, WIDTH)


SB_DEAD = -104.0


def _sb_step(qz, k_pairs, v_pairs, tri, carry_scr, acc_scr, mask):
    tq = qz[0].shape[0]
    live = None
    for h in range(N_HEADS):
        z = lax.dot_general(qz[h], k_pairs[h // 2], _NT, preferred_element_type=F32)
        lg = jnp.log(1.0 + jnp.exp(-jnp.abs(z)))
        log1m = -(jnp.maximum(z, 0.0) + lg)
        if mask is not None:
            log1m = jnp.where(mask, log1m, 0.0)
        logsig = jnp.minimum(z, 0.0) - lg
        hi = log1m.astype(BF16)
        lo = (log1m - hi.astype(F32)).astype(BF16)
        after = jnp.dot(jnp.concatenate([hi, lo], axis=1), tri, preferred_element_type=F32)
        carry = carry_scr[h]
        a = jnp.concatenate([jnp.exp(t + carry) for t in _lane_tiles(logsig + after)], axis=1)
        if mask is not None:
            a = jnp.where(mask, a, 0.0)
        acc_scr[h] += jnp.dot(a.astype(BF16), v_pairs[h // 2], preferred_element_type=F32)
        carry = carry + jnp.broadcast_to(jnp.sum(log1m, axis=1, keepdims=True), (tq, LANES))
        carry_scr[h] = carry
        live = carry if live is None else jnp.maximum(live, carry)
    return (jnp.max(live) > SB_DEAD).astype(I32)


def _sb_finish(acc_scr, gt):
    lo = _head_masks()
    outs = [jnp.where(lo, acc_scr[2 * p], acc_scr[2 * p + 1]) for p in range(PAIRS)]
    return (jnp.concatenate(outs, axis=1) * gt.astype(F32)).astype(BF16)


def _sb_tri(kblk):
    jj = np.arange(kblk)
    m = (jj[:, None] > jj[None, :])
    return jnp.asarray(np.concatenate([m, m], axis=0), BF16)


def _sb_scratch(tq):
    return [pltpu.VMEM((N_HEADS, tq, LANES), F32), pltpu.VMEM((N_HEADS, tq, LANES), F32)]


def _sb_prompt_kernel(qb_ref, gt_ref, kb_ref, vb_ref, tri_ref, o_ref, carry_scr, acc_scr, *, tq):
    j = pl.program_id(1)
    qz = _masked_heads(qb_ref[0])
    tri = tri_ref[...]
    diag = (lax.broadcasted_iota(I32, (tq, tq), 1) < lax.broadcasted_iota(I32, (tq, tq), 0))
    carry_scr[...] = jnp.zeros(carry_scr.shape, F32)
    acc_scr[...] = jnp.zeros(acc_scr.shape, F32)

    def step(kb, mask):
        start = pl.multiple_of(kb * tq, tq)
        return _sb_step(qz, _pair_blocks(kb_ref, start, tq), _pair_blocks(vb_ref, start, tq),
                        tri, carry_scr, acc_scr, mask)

    live = step(j, diag)
    lax.while_loop(lambda st: (st[0] >= 0) & (st[1] > 0),
                   lambda st: (st[0] - 1, step(st[0], None)), (j - 1, live))
    o_ref[0] = _sb_finish(acc_scr, gt_ref[0])


def _sb_prompt(qb, gtb, kbb, vbb, batch, seq):
    tq = min(SB_TQ, seq)
    assert seq % tq == 0
    r3 = lambda a: a.reshape(batch, seq, a.shape[-1])
    qblk = pl.BlockSpec((1, tq, WIDTH), lambda b, j: (b, j, 0))
    full = pl.BlockSpec((1, seq, WIDTH), lambda b, j: (b, 0, 0))
    out = pl.pallas_call(
        functools.partial(_sb_prompt_kernel, tq=tq),
        grid=(batch, seq // tq),
        in_specs=[qblk, qblk, full, full, pl.BlockSpec((2 * tq, tq), lambda b, j: (0, 0))],
        out_specs=qblk,
        out_shape=jax.ShapeDtypeStruct((batch, seq, WIDTH), BF16),
        scratch_shapes=_sb_scratch(tq),
        compiler_params=_params("parallel", "arbitrary"),
    )(r3(qb), r3(gtb), r3(kbb), r3(vbb), _sb_tri(tq))
    return out.reshape(batch * seq, WIDTH)


def _sb_sample_kernel(qb_ref, gt_ref, kb_ref, vb_ref, ckb_ref, cvb_ref, tri_ref, o_ref,
                      carry_scr, acc_scr, *, tq, kblk, past):
    ncache = past // kblk
    qz = _masked_heads(qb_ref[0])
    tri = tri_ref[...]
    diag = (lax.broadcasted_iota(I32, (tq, kblk), 1) < lax.broadcasted_iota(I32, (tq, kblk), 0))
    carry_scr[...] = jnp.zeros(carry_scr.shape, F32)
    acc_scr[...] = jnp.zeros(acc_scr.shape, F32)
    live = _sb_step(qz, [_pad_rows(t, kblk) for t in _lane_tiles(kb_ref[0])],
                    [_pad_rows(t, kblk) for t in _lane_tiles(vb_ref[0])],
                    tri, carry_scr, acc_scr, diag)

    def step(kb):
        start = pl.multiple_of(kb * kblk, kblk)
        return _sb_step(qz, _pair_blocks(ckb_ref, start, kblk, cast=True),
                        _pair_blocks(cvb_ref, start, kblk, cast=True), tri, carry_scr, acc_scr, None)

    lax.while_loop(lambda st: (st[0] >= 0) & (st[1] > 0),
                   lambda st: (st[0] - 1, step(st[0])), (jnp.int32(ncache - 1), live))
    o_ref[0] = _sb_finish(acc_scr, gt_ref[0])


def _sb_sample(qb, gtb, kbb, vbb, cache_k, cache_v, batch, tq, past):
    kblk = min(KEY_BLOCK, past)
    assert past % kblk == 0 and tq <= kblk
    r3 = lambda a: a.reshape(batch, tq, a.shape[-1])
    new = pl.BlockSpec((1, tq, WIDTH), lambda b: (b, 0, 0))
    cache = pl.BlockSpec((1, past, WIDTH), lambda b: (b, 0, 0))
    out = pl.pallas_call(
        functools.partial(_sb_sample_kernel, tq=tq, kblk=kblk, past=past),
        grid=(batch,),
        in_specs=[new, new, new, new, cache, cache, pl.BlockSpec((2 * kblk, kblk), lambda b: (0, 0))],
        out_specs=new,
        out_shape=jax.ShapeDtypeStruct((batch, tq, WIDTH), BF16),
        scratch_shapes=_sb_scratch(tq),
        compiler_params=_params("parallel"),
    )(r3(qb), r3(gtb), r3(kbb), r3(vbb), cache_k.reshape(batch, past, WIDTH),
      cache_v.reshape(batch, past, WIDTH), _sb_tri(kblk))
    return out.reshape(batch * tq, WIDTH)


def _merge_kernel(x_ref, ta_ref, tb_ref, sga_ref, sgb_ref, wa_ref, wb_ref, wo_ref, y_ref):
    ya = jnp.dot(ta_ref[...], wa_ref[...], preferred_element_type=F32)
    yb = jnp.dot(tb_ref[...], wb_ref[...], preferred_element_type=F32)
    mixed = sga_ref[...].astype(F32) * ya + sgb_ref[...].astype(F32) * yb
    y_ref[...] = x_ref[...] + jnp.dot(mixed.astype(BF16), wo_ref[...], preferred_element_type=F32)


def _merge(x2d, ta, tb, sga, sgb, wa, wb, wo):
    n, d_model = x2d.shape
    rb = min(ROW_BLOCK, n)
    row = lambda w: pl.BlockSpec((rb, w), lambda i: (i, 0))
    const = lambda a: pl.BlockSpec(a.shape, lambda i: (0, 0))
    return pl.pallas_call(
        _merge_kernel,
        grid=(n // rb,),
        in_specs=[row(d_model), row(WIDTH), row(WIDTH), row(d_model), row(d_model),
                  const(wa), const(wb), const(wo)],
        out_specs=row(d_model),
        out_shape=jax.ShapeDtypeStruct((n, d_model), F32),
        compiler_params=_params("parallel"),
    )(x2d, ta, tb, sga, sgb, wa, wb, wo)


def _layer(x, pos, past, params):
    norm_g, w_pack, q_norm_g, k_norm_g, idx_k_norm_g, wa, wb, wo = params
    b, t, d_model = x.shape
    x2d = x.reshape(b * t, d_model)
    (qa, ka, kab, va, vab, gta, qi, ki, kib, wi, qb, kb, kbb, vb, vbb, gtb, sga, sgb) = _project(
        x2d, pos, t, norm_g, w_pack, q_norm_g, k_norm_g, idx_k_norm_g)
    if past is None:
        ta = _dsa_prompt(qa, qi, wi, gta, kab, vab, kib, b, t)
        tb = _sb_prompt(qb, gtb, kbb, vbb, b, t)
    else:
        c_ak, c_av, c_ik, c_bk, c_bv = past
        p_len = c_ak.shape[1]
        ta = _dsa_sample(qa, qi, wi, gta, kab, vab, kib, c_ak, c_av, c_ik, b, t, p_len)
        tb = _sb_sample(qb, gtb, kbb, vbb, c_bk, c_bv, b, t, p_len)
    y = _merge(x2d, ta, tb, sga, sgb, wa, wb, wo).reshape(b, t, d_model)
    rows = (ka.reshape(b, t, N_HEADS, HEAD_DIM), va.reshape(b, t, N_HEADS, HEAD_DIM),
            ki.reshape(b, t, IDX_DIM), kb.reshape(b, t, N_HEADS, HEAD_DIM),
            vb.reshape(b, t, N_HEADS, HEAD_DIM))
    return y, rows


def kernel(x_prompt, x_sample, cache_a_k, cache_a_v, cache_idx_k, cache_b_k, cache_b_v,
           norm_g, w_in, q_norm_g, k_norm_g, idx_k_norm_g, w_a_out, w_b_out, w_o):
    depth = norm_g.shape[0]
    d_model = x_prompt.shape[2]
    seq = x_prompt.shape[1]
    past_len = cache_a_k.shape[2]
    dec_seq = x_sample.shape[1]
    pos_p = jnp.arange(seq, dtype=I32)
    pos_s = past_len + jnp.arange(dec_seq, dtype=I32)
    yp, ys = x_prompt, x_sample
    new_p, new_s = [], []
    for l in range(depth):
        params = (norm_g[l], _pack_w_in(w_in[l], d_model), q_norm_g[l], k_norm_g[l], idx_k_norm_g[l],
                  w_a_out[l].astype(BF16), w_b_out[l].astype(BF16), w_o[l].astype(BF16))
        yp, rows_p = _layer(yp, pos_p, None, params)
        past = (cache_a_k[l], cache_a_v[l], cache_idx_k[l], cache_b_k[l], cache_b_v[l])
        ys, rows_s = _layer(ys, pos_s, past, params)
        new_p.append(rows_p)
        new_s.append(rows_s)
    stk = lambda rows, i: jnp.stack([r[i] for r in rows], axis=0)
    return (yp, ys,
            stk(new_p, 0), stk(new_p, 1), stk(new_p, 2), stk(new_p, 3), stk(new_p, 4),
            stk(new_s, 0), stk(new_s, 1), stk(new_s, 2), stk(new_s, 3), stk(new_s, 4))
```

```python
import functools

import numpy as np
import jax
import jax.numpy as jnp
from jax import lax
from jax.experimental import pallas as pl
from jax.experimental.pallas import tpu as pltpu

F32 = jnp.float32
BF16 = jnp.bfloat16
I32 = jnp.int32
I16 = jnp.int16

HEAD_DIM = 64
N_HEADS = 8
WIDTH = N_HEADS * HEAD_DIM
IDX_DIM = 64
CHUNK = 64
TOPK_MAX = 256
ROPE_THETA = 500000.0
EPS = 1e-6
NEG_INF = -1e30

LANES = 128
SUBLANES = 8
PACKED_ROWS = 16
HALF_BIAS = 1 << 15
PAIRS = WIDTH // LANES
VMEM_LIMIT_BYTES = 56 * 1024 * 1024

INT_MIN = -(2 ** 31)
INT_MAX = 2 ** 31 - 1
_NEG_BITS = int(np.array(NEG_INF, np.float32).view(np.int32))
NEG_KEY = _NEG_BITS ^ 0x7FFFFFFF

ROW_BLOCK = 256
KEY_BLOCK = 256
SB_TQ = 256
SCORE_ROWS = 128

_NT = (((1,), (1,)), ((), ()))


def _params(*sem):
    return pltpu.CompilerParams(dimension_semantics=sem, vmem_limit_bytes=VMEM_LIMIT_BYTES)


_G_QA, _G_KA, _G_VA, _G_UA, _G_QI = (i * WIDTH for i in range(5))
_G_KI = 5 * WIDTH
_G_WI = _G_KI + LANES
_G_QB = _G_WI + LANES
_G_KB, _G_VB, _G_UB = (_G_QB + i * WIDTH for i in range(1, 4))
_G_GA = _G_QB + 4 * WIDTH


def _rope(x, cs, s1, s2):
    outs = []
    for j in range(x.shape[1] // LANES):
        xj = x[:, j * LANES:(j + 1) * LANES]
        outs.append(xj * cs + pltpu.roll(xj, LANES - 8, 1) * s1 + pltpu.roll(xj, 8, 1) * s2)
    return outs[0] if len(outs) == 1 else jnp.concatenate(outs, axis=1)


def _proj_kernel(x_ref, g_ref, w_ref, cs_ref, s1_ref, s2_ref, qg_ref, kg_ref, ig_ref, bd_ref,
                 qa_o, ka_o, kab_o, va_o, vab_o, gta_o, qi_o, ki_o, kib_o, wi_o,
                 qb_o, kb_o, kbb_o, vb_o, vbb_o, gtb_o, sga_o, sgb_o, vat_o, *, d_model):
    x = x_ref[...]
    ms = jnp.mean(x * x, axis=-1, keepdims=True)
    xn = (x * lax.rsqrt(ms + EPS) * g_ref[...]).astype(BF16)
    cs, s1, s2 = cs_ref[...], s1_ref[...], s2_ref[...]

    def proj(c0, width):
        return jnp.dot(xn, w_ref[:, c0:c0 + width], preferred_element_type=F32)

    def head_norm(h, gain):
        ss = jnp.dot((h * h).astype(BF16), bd_ref[...], preferred_element_type=F32)
        return h * lax.rsqrt(ss * (1.0 / HEAD_DIM) + EPS) * gain

    qa = _rope(head_norm(proj(_G_QA, WIDTH), qg_ref[...]), cs, s1, s2)
    qa_o[...] = (qa * (HEAD_DIM ** -0.5)).astype(BF16)
    ka = _rope(head_norm(proj(_G_KA, WIDTH), kg_ref[...]), cs, s1, s2)
    ka_o[...] = ka
    kab_o[...] = ka.astype(BF16)
    va = proj(_G_VA, WIDTH)
    va_o[...] = va
    vab_o[...] = va.astype(BF16)
    vat_o[0] = va.T.astype(BF16)
    ua = proj(_G_UA, WIDTH)
    gta_o[...] = (ua * jax.nn.sigmoid(ua)).astype(BF16)
    qi_o[...] = _rope(proj(_G_QI, WIDTH), cs, s1, s2).astype(BF16)

    kw = proj(_G_KI, 2 * LANES)
    hk = kw[:, :LANES]
    ssk = jnp.sum(hk * hk, axis=-1, keepdims=True)
    kin = hk * lax.rsqrt(ssk * (1.0 / IDX_DIM) + EPS) * ig_ref[...]
    ki = _rope(kin, cs, s1, s2)[:, :IDX_DIM]
    ki_o[...] = ki
    kib_o[...] = ki.astype(BF16)
    wi_o[...] = kw[:, LANES:] * (N_HEADS ** -0.5) * (IDX_DIM ** -0.5)

    qb_o[...] = (proj(_G_QB, WIDTH) * (HEAD_DIM ** -0.5)).astype(BF16)
    kb = proj(_G_KB, WIDTH)
    kb_o[...] = kb
    kbb_o[...] = kb.astype(BF16)
    vb = proj(_G_VB, WIDTH)
    vb_o[...] = vb
    vbb_o[...] = vb.astype(BF16)
    ub = proj(_G_UB, WIDTH)
    gtb_o[...] = (ub * jax.nn.sigmoid(ub)).astype(BF16)
    sga_o[...] = jax.nn.sigmoid(proj(_G_GA, d_model)).astype(BF16)
    sgb_o[...] = jax.nn.sigmoid(proj(_G_GA + d_model, d_model)).astype(BF16)


def _pack_w_in(w_in, d_model):
    o = 0
    parts = []

    def take(n, pad_to=None):
        nonlocal o
        blk = w_in[:, o:o + n]
        o += n
        if pad_to is not None and pad_to > n:
            blk = jnp.pad(blk, ((0, 0), (0, pad_to - n)))
        parts.append(blk)

    for _ in range(5):
        take(WIDTH)
    take(IDX_DIM, LANES)
    take(N_HEADS, LANES)
    for _ in range(4):
        take(WIDTH)
    take(d_model)
    take(d_model)
    assert o == w_in.shape[1]
    return jnp.concatenate(parts, axis=1).astype(BF16)


def _rope_tables(pos):
    half = HEAD_DIM // 8
    inv_freq = ROPE_THETA ** (-jnp.arange(half, dtype=F32) / half)
    ang = pos.astype(F32)[:, None] * inv_freq[None, :]
    cos, sin = jnp.cos(ang), jnp.sin(ang)
    t = pos.shape[0]
    ones = jnp.ones((t, HEAD_DIM - 2 * half), F32)
    zeros = jnp.zeros((t, HEAD_DIM - 2 * half), F32)
    z8 = jnp.zeros((t, half), F32)
    cs = jnp.concatenate([cos, cos, ones], axis=1)
    s1 = jnp.concatenate([-sin, z8, zeros], axis=1)
    s2 = jnp.concatenate([z8, sin, zeros], axis=1)
    rep = lambda a: jnp.concatenate([a, a], axis=1)
    return rep(cs), rep(s1), rep(s2)


def _project(x2d, pos, t_len, norm_g, w_pack, q_norm_g, k_norm_g, idx_k_norm_g):
    n, d_model = x2d.shape
    rb = min(ROW_BLOCK, n)
    assert n % rb == 0
    cs, s1, s2 = _rope_tables(pos)
    if t_len >= rb:
        assert t_len % rb == 0
        tb = t_len // rb
        tab_map = lambda i: (i % tb, 0)
    else:
        assert rb % t_len == 0
        cs, s1, s2 = (jnp.tile(a, (rb // t_len, 1)) for a in (cs, s1, s2))
        tab_map = lambda i: (0, 0)
    qg = jnp.tile(q_norm_g, N_HEADS)[None, :]
    kg = jnp.tile(k_norm_g, N_HEADS)[None, :]
    ig = jnp.pad(idx_k_norm_g, (0, LANES - IDX_DIM))[None, :]
    hid = np.arange(WIDTH) // HEAD_DIM
    bd = jnp.asarray(hid[:, None] == hid[None, :], BF16)
    e_pack = w_pack.shape[1]

    row = lambda w: pl.BlockSpec((rb, w), lambda i: (i, 0))
    const = lambda shape: pl.BlockSpec(shape, lambda i: (0, 0))
    f32o = lambda w: jax.ShapeDtypeStruct((n, w), F32)
    b16o = lambda w: jax.ShapeDtypeStruct((n, w), BF16)
    outs = [
        (b16o(WIDTH), row(WIDTH)),
        (f32o(WIDTH), row(WIDTH)),
        (b16o(WIDTH), row(WIDTH)),
        (f32o(WIDTH), row(WIDTH)),
        (b16o(WIDTH), row(WIDTH)),
        (b16o(WIDTH), row(WIDTH)),
        (b16o(WIDTH), row(WIDTH)),
        (f32o(IDX_DIM), row(IDX_DIM)),
        (b16o(IDX_DIM), row(IDX_DIM)),
        (f32o(LANES), row(LANES)),
        (b16o(WIDTH), row(WIDTH)),
        (f32o(WIDTH), row(WIDTH)),
        (b16o(WIDTH), row(WIDTH)),
        (f32o(WIDTH), row(WIDTH)),
        (b16o(WIDTH), row(WIDTH)),
        (b16o(WIDTH), row(WIDTH)),
        (b16o(d_model), row(d_model)),
        (b16o(d_model), row(d_model)),
        (jax.ShapeDtypeStruct((n // rb, WIDTH, rb), BF16),
         pl.BlockSpec((1, WIDTH, rb), lambda i: (i, 0, 0))),
    ]
    return pl.pallas_call(
        functools.partial(_proj_kernel, d_model=d_model),
        grid=(n // rb,),
        in_specs=[row(d_model), const((1, d_model)), const((d_model, e_pack)),
                  pl.BlockSpec((rb, LANES), tab_map), pl.BlockSpec((rb, LANES), tab_map),
                  pl.BlockSpec((rb, LANES), tab_map),
                  const((1, WIDTH)), const((1, WIDTH)), const((1, LANES)), const((WIDTH, WIDTH))],
        out_specs=[o[1] for o in outs],
        out_shape=[o[0] for o in outs],
        compiler_params=_params("parallel"),
    )(x2d, norm_g[None, :], w_pack, cs, s1, s2, qg, kg, ig, bd)


def _sort_key(score):
    b = lax.bitcast_convert_type(score, I32)
    b = jnp.where(b == INT_MIN, 0, b)
    return jnp.where(b < 0, b ^ INT_MAX, b)


def _lane_tiles(x):
    return [x[:, c * LANES:(c + 1) * LANES] for c in range(x.shape[1] // LANES)]


def _chunk_limit(pos):
    shift = CHUNK.bit_length() - 1
    assert CHUNK == 1 << shift
    return ((pos >> shift) + 1) << shift


def _head_masks():
    lane = lax.broadcasted_iota(I32, (1, LANES), 1)
    return lane < HEAD_DIM


def _masked_heads(q):
    lo = _head_masks()
    out = []
    for qp in _lane_tiles(q):
        zero = jnp.zeros_like(qp)
        out += [jnp.where(lo, qp, zero), jnp.where(lo, zero, qp)]
    return out


def _rows8(x):
    return x.reshape(x.shape[0] // SUBLANES, SUBLANES, x.shape[1])


def _fold_rows(x, op):
    return op(_rows8(x), axis=0)


def _index_queries(qi, wip):
    q_heads = [qi[:, h * IDX_DIM:(h + 1) * IDX_DIM] for h in range(N_HEADS)]
    wt = wip.T
    tq = qi.shape[0]
    return q_heads, [jnp.broadcast_to(wt[h:h + 1, :], (SUBLANES, tq)) for h in range(N_HEADS)]


def _block_scores(ki_blk, q_heads, w_rows):
    parts = []
    for r0 in range(0, ki_blk.shape[0], SCORE_ROWS):
        sc = None
        for h in range(N_HEADS):
            d = lax.dot_general(ki_blk[r0:r0 + SCORE_ROWS], q_heads[h], _NT,
                                preferred_element_type=F32)
            t = jnp.maximum(_rows8(d), 0.0) * w_rows[h][None]
            sc = t if sc is None else sc + t
        parts.append(sc.reshape(SCORE_ROWS, sc.shape[2]))
    return jnp.concatenate(parts, axis=0)


def _select_bias(keys_scr, half_scr, bias_scr, nkb, limit, topk, n_unprocessed, real=None):
    _, kblk, tq = keys_scr.shape

    def count(pred):
        def body(kb, acc):
            return acc + _fold_rows(jnp.where(pred(keys_scr[kb], kb), 1, 0), jnp.sum)
        acc = lax.fori_loop(0, nkb, body, jnp.zeros((SUBLANES, tq), I32))
        return jnp.sum(acc, axis=0, keepdims=True)

    def count_ge(cand):
        c = count(lambda k, kb: k >= cand)
        return c + jnp.where(cand <= NEG_KEY, n_unprocessed, 0)

    hi_scr, lo_scr = half_scr.at[0], half_scr.at[1]
    neg_hi, neg_lo = NEG_KEY >> 16, (NEG_KEY & 0xFFFF) - HALF_BIAS

    def split(kb, c):
        k = keys_scr[kb]
        hi_scr[kb] = (k >> 16).astype(I16)
        lo_scr[kb] = ((k & 0xFFFF) - HALF_BIAS).astype(I16)
        return c

    lax.fori_loop(0, nkb, split, 0)

    def count16(scr, cand):
        c16 = cand.astype(I16)

        def body(kb, acc):
            hit = jnp.where(scr[kb] >= c16, jnp.int16(1), jnp.int16(0))
            tiles = [hit[r:r + PACKED_ROWS] for r in range(0, kblk, PACKED_ROWS)]
            return acc + functools.reduce(jnp.add, tiles)
        acc = lax.fori_loop(0, nkb, body, jnp.zeros((PACKED_ROWS, tq), I16))
        return jnp.sum(acc.astype(I32), axis=0, keepdims=True)

    def search16(scr, target, extra):
        def step(i, t):
            cand = t + lax.shift_left(jnp.int32(1), 15 - i)
            return jnp.where(count16(scr, cand) + extra(cand) >= target, cand, t)
        return lax.fori_loop(0, 16, step, jnp.full((1, tq), -HALF_BIAS, I32))

    hi_extra = lambda cand: jnp.where(cand <= neg_hi, n_unprocessed, 0)
    thr_hi = search16(hi_scr, topk, hi_extra)
    above = count16(hi_scr, thr_hi + 1) + hi_extra(thr_hi + 1)
    thr_hi16 = thr_hi.astype(I16)

    def keep_matching(kb, c):
        lo_scr[kb] = jnp.where(hi_scr[kb] == thr_hi16, lo_scr[kb], jnp.int16(-HALF_BIAS))
        return c

    lax.fori_loop(0, nkb, keep_matching, 0)
    lo_extra = lambda cand: jnp.where((thr_hi == neg_hi) & (cand <= neg_lo), n_unprocessed, 0)
    thr_lo = search16(lo_scr, topk - above, lo_extra)
    thr = lax.shift_left(thr_hi, 16) + (thr_lo + HALF_BIAS)
    c_gt = count_ge(thr + 1)
    c_ge = count_ge(thr)
    need = topk - c_gt
    surplus = c_ge - c_gt > need
    if real is not None:
        surplus = surplus & real
    excess = jnp.max(jnp.where(surplus, 1, 0)) > 0

    def key_index(kb):
        return kb * kblk + lax.broadcasted_iota(I32, (kblk, tq), 0)

    nbits = int(keys_scr.shape[0] * kblk).bit_length() + 1

    def tie_cut():
        def step(i, cut):
            cand = cut + lax.shift_left(jnp.int32(1), nbits - 1 - i)
            c = count(lambda k, kb: (k == thr) & (key_index(kb) < cand))
            return jnp.where(c <= need, cand, cut)
        return lax.fori_loop(0, nbits, step, jnp.zeros((1, tq), I32))

    cut = lax.cond(excess, tie_cut, lambda: jnp.full((1, tq), INT_MAX, I32))

    def write(kb, c):
        k = keys_scr[kb]
        idx = key_index(kb)
        sel = ((k > thr) | ((k == thr) & (idx < cut))) & (idx < limit)
        bias_scr[kb] = jnp.where(sel, 0.0, NEG_INF)
        return c

    lax.fori_loop(0, nkb, write, 0)


def _attention(qa, loop_n, load, tail, bias_scr, lg_scr, m_scr, mn_scr, l_scr, acc_scr):
    qz = _masked_heads(qa)
    m_scr[...] = jnp.full(m_scr.shape, NEG_INF, F32)
    l_scr[...] = jnp.zeros(l_scr.shape, F32)
    acc_scr[...] = jnp.zeros(acc_scr.shape, F32)

    def step(kb, k_pairs, vt):
        bias = bias_scr[kb]
        for h in range(N_HEADS):
            lg = lax.dot_general(k_pairs[h // 2], qz[h], _NT, preferred_element_type=F32) + bias
            lg_scr[h] = lg
            blk_max = jnp.max(_fold_rows(lg, jnp.max), axis=0, keepdims=True)
            mn_scr[h] = jnp.maximum(m_scr[h], blk_max)
        for h in range(N_HEADS):
            m_new = mn_scr[h]
            alpha = jnp.exp(m_scr[h] - m_new)
            p = jnp.exp(_rows8(lg_scr[h]) - m_new[None])
            l_scr[h] = alpha * l_scr[h] + jnp.sum(jnp.sum(p, axis=0), axis=0, keepdims=True)
            pv = jnp.dot(vt[h * HEAD_DIM:(h + 1) * HEAD_DIM, :],
                         p.reshape(lg_scr.shape[1:]).astype(BF16),
                         preferred_element_type=F32)
            acc_scr[h] = (_rows8(acc_scr[h]) * alpha[None]).reshape(pv.shape) + pv
            m_scr[h] = m_new

    def body(kb, c):
        step(kb, *load(kb))
        return c

    lax.fori_loop(0, loop_n, body, 0)
    if tail is not None:
        step(*tail)
    outs = [(_rows8(acc_scr[h]) / l_scr[h][None]).reshape(acc_scr.shape[1:]) for h in range(N_HEADS)]
    return jnp.concatenate(outs, axis=0).T


def _attention_scratch(nkb, tq, kblk):
    return [pltpu.VMEM((nkb, kblk, tq), I32),
            pltpu.VMEM((2, nkb, kblk, tq), I16),
            pltpu.VMEM((nkb, kblk, tq), F32),
            pltpu.VMEM((N_HEADS, kblk, tq), F32),
            pltpu.VMEM((N_HEADS, SUBLANES, tq), F32),
            pltpu.VMEM((N_HEADS, SUBLANES, tq), F32),
            pltpu.VMEM((N_HEADS, SUBLANES, tq), F32),
            pltpu.VMEM((N_HEADS, HEAD_DIM, tq), F32)]


def _pair_blocks(ref, start, rows, cast=False):
    out = [ref[0, pl.ds(start, rows), p * LANES:(p + 1) * LANES] for p in range(PAIRS)]
    return [o.astype(BF16) for o in out] if cast else out


def _dsa_prompt_kernel(qa_ref, qi_ref, wi_ref, gt_ref, ka_ref, vt_ref, ki_ref, o_ref,
                       keys_scr, half_scr, bias_scr, lg_scr, m_scr, mn_scr, l_scr, acc_scr, *, tq, topk, seq):
    kblk = tq
    j = pl.program_id(1)
    nkb = j + 1
    pos = j * tq + lax.broadcasted_iota(I32, (1, tq), 1)
    limit = _chunk_limit(pos)
    q_heads, w_rows = _index_queries(qi_ref[0], wi_ref[0])

    def score_body(kb, c):
        start = pl.multiple_of(kb * kblk, kblk)
        sc = _block_scores(ki_ref[0, pl.ds(start, kblk), :], q_heads, w_rows)
        kpos = kb * kblk + lax.broadcasted_iota(I32, (kblk, tq), 0)
        keys_scr[kb] = _sort_key(jnp.where(kpos < limit, sc, NEG_INF))
        return c

    lax.fori_loop(0, nkb, score_body, 0)
    _select_bias(keys_scr, half_scr, bias_scr,nkb, limit, topk, seq - nkb * kblk)

    def load(kb):
        return _pair_blocks(ka_ref, pl.multiple_of(kb * kblk, kblk), kblk), vt_ref[0, kb]

    oa = _attention(qa_ref[0], nkb, load, None, bias_scr, lg_scr, m_scr, mn_scr, l_scr, acc_scr)
    o_ref[0] = (oa * gt_ref[0].astype(F32)).astype(BF16)


def _dsa_prompt(qa, qi, wip, gta, kab, vat, kib, batch, seq):
    tq = ROW_BLOCK
    assert seq % tq == 0 and tq % CHUNK == 0
    topk = min(TOPK_MAX, seq // 4)
    nblk = seq // tq
    r3 = lambda a: a.reshape(batch, seq, a.shape[-1])
    qblk = lambda w: pl.BlockSpec((1, tq, w), lambda b, j: (b, j, 0))
    full = lambda w: pl.BlockSpec((1, seq, w), lambda b, j: (b, 0, 0))
    out = pl.pallas_call(
        functools.partial(_dsa_prompt_kernel, tq=tq, topk=topk, seq=seq),
        grid=(batch, nblk),
        in_specs=[qblk(WIDTH), qblk(WIDTH), qblk(LANES), qblk(WIDTH), full(WIDTH),
                  pl.BlockSpec((1, nblk, WIDTH, tq), lambda b, j: (b, 0, 0, 0)), full(IDX_DIM)],
        out_specs=qblk(WIDTH),
        out_shape=jax.ShapeDtypeStruct((batch, seq, WIDTH), BF16),
        scratch_shapes=_attention_scratch(nblk, tq, tq),
        compiler_params=_params("parallel", "arbitrary"),
    )(r3(qa), r3(qi), r3(wip), r3(gta), r3(kab), vat.reshape(batch, nblk, WIDTH, tq), r3(kib))
    return out.reshape(batch * seq, WIDTH)


def _pad_rows(x, rows):
    return jnp.concatenate([x, jnp.zeros((rows - x.shape[0], x.shape[1]), x.dtype)], axis=0)


def _dsa_sample_kernel(qa_ref, qi_ref, wi_ref, gt_ref, ka_ref, va_ref, ki_ref,
                       cka_ref, cva_ref, cki_ref, o_ref,
                       keys_scr, half_scr, bias_scr, lg_scr, m_scr, mn_scr, l_scr, acc_scr, *, tq, tqp, kblk, topk, past):
    ncache = past // kblk
    total = past + tq
    pos = past + lax.broadcasted_iota(I32, (1, tqp), 1)
    limit = jnp.minimum(_chunk_limit(pos), total)
    q_heads, w_rows = _index_queries(_pad_rows(qi_ref[0], tqp), _pad_rows(wi_ref[0], tqp))

    def to_keys(sc, kb):
        kpos = kb * kblk + lax.broadcasted_iota(I32, (kblk, tqp), 0)
        keys = _sort_key(jnp.where(kpos < limit, sc, NEG_INF))
        return jnp.where(kpos < total, keys, INT_MIN)

    def cache_body(kb, c):
        start = pl.multiple_of(kb * kblk, kblk)
        ki_blk = cki_ref[0, pl.ds(start, kblk), :].astype(BF16)
        keys_scr[kb] = to_keys(_block_scores(ki_blk, q_heads, w_rows), kb)
        return c

    lax.fori_loop(0, ncache, cache_body, 0)
    keys_scr[ncache] = to_keys(_block_scores(_pad_rows(ki_ref[0], kblk), q_heads, w_rows), ncache)
    _select_bias(keys_scr, half_scr, bias_scr,ncache + 1, limit, topk, 0, real=pos < total)

    def load(kb):
        start = pl.multiple_of(kb * kblk, kblk)
        vt = cva_ref[0, pl.ds(start, kblk), :].T.astype(BF16)
        return _pair_blocks(cka_ref, start, kblk, cast=True), vt

    tail = (ncache, [_pad_rows(t, kblk) for t in _lane_tiles(ka_ref[0])],
            _pad_rows(va_ref[0], kblk).astype(F32).T.astype(BF16))
    oa = _attention(_pad_rows(qa_ref[0], tqp), ncache, load, tail, bias_scr, lg_scr, m_scr, mn_scr, l_scr, acc_scr)
    o_ref[0] = (oa[:tq] * gt_ref[0].astype(F32)).astype(BF16)


def _dsa_sample(qa, qi, wip, gta, kab, vab, kib, cache_k, cache_v, cache_ki, batch, tq, past):
    kblk = min(KEY_BLOCK, past)
    tqp = -(-tq // LANES) * LANES
    assert past % kblk == 0 and tq <= kblk
    topk = min(TOPK_MAX, (past + tq) // 4)
    assert past + tq >= topk
    r3 = lambda a: a.reshape(batch, tq, a.shape[-1])
    new = lambda w: pl.BlockSpec((1, tq, w), lambda b: (b, 0, 0))
    cache = lambda w: pl.BlockSpec((1, past, w), lambda b: (b, 0, 0))
    out = pl.pallas_call(
        functools.partial(_dsa_sample_kernel, tq=tq, tqp=tqp, kblk=kblk, topk=topk, past=past),
        grid=(batch,),
        in_specs=[new(WIDTH), new(WIDTH), new(LANES), new(WIDTH),
                  new(WIDTH), new(WIDTH), new(IDX_DIM),
                  cache(WIDTH), cache(WIDTH), cache(IDX_DIM)],
        out_specs=new(WIDTH),
        out_shape=jax.ShapeDtypeStruct((batch, tq, WIDTH), BF16),
        scratch_shapes=_attention_scratch(past // kblk + 1, tqp, kblk),
        compiler_params=_params("parallel"),
    )(r3(qa), r3(qi), r3(wip), r3(gta), r3(kab), r3(vab), r3(kib),
      cache_k.reshape(batch, past, WIDTH), cache_v.reshape(batch, past, WIDTH), cache_ki)
    return out.reshape(batch * tq, WIDTH)


SB_DEAD = -104.0


def _sb_step(qz, k_pairs, v_pairs, tri, carry_scr, acc_scr, mask):
    tq = qz[0].shape[0]
    live = None
    for h in range(N_HEADS):
        z = lax.dot_general(qz[h], k_pairs[h // 2], _NT, preferred_element_type=F32)
        lg = jnp.log(1.0 + jnp.exp(-jnp.abs(z)))
        log1m = -(jnp.maximum(z, 0.0) + lg)
        if mask is not None:
            log1m = jnp.where(mask, log1m, 0.0)
        logsig = jnp.minimum(z, 0.0) - lg
        hi = log1m.astype(BF16)
        lo = (log1m - hi.astype(F32)).astype(BF16)
        after = jnp.dot(jnp.concatenate([hi, lo], axis=1), tri, preferred_element_type=F32)
        carry = carry_scr[h]
        a = jnp.concatenate([jnp.exp(t + carry) for t in _lane_tiles(logsig + after)], axis=1)
        if mask is not None:
            a = jnp.where(mask, a, 0.0)
        acc_scr[h] += jnp.dot(a.astype(BF16), v_pairs[h // 2], preferred_element_type=F32)
        carry = carry + jnp.broadcast_to(jnp.sum(log1m, axis=1, keepdims=True), (tq, LANES))
        carry_scr[h] = carry
        live = carry if live is None else jnp.maximum(live, carry)
    return (jnp.max(live) > SB_DEAD).astype(I32)


def _sb_finish(acc_scr, gt):
    lo = _head_masks()
    outs = [jnp.where(lo, acc_scr[2 * p], acc_scr[2 * p + 1]) for p in range(PAIRS)]
    return (jnp.concatenate(outs, axis=1) * gt.astype(F32)).astype(BF16)


def _sb_tri(kblk):
    jj = np.arange(kblk)
    m = (jj[:, None] > jj[None, :])
    return jnp.asarray(np.concatenate([m, m], axis=0), BF16)


def _sb_scratch(tq):
    return [pltpu.VMEM((N_HEADS, tq, LANES), F32), pltpu.VMEM((N_HEADS, tq, LANES), F32)]


def _sb_prompt_kernel(qb_ref, gt_ref, kb_ref, vb_ref, tri_ref, o_ref, carry_scr, acc_scr, *, tq):
    j = pl.program_id(1)
    qz = _masked_heads(qb_ref[0])
    tri = tri_ref[...]
    diag = (lax.broadcasted_iota(I32, (tq, tq), 1) < lax.broadcasted_iota(I32, (tq, tq), 0))
    carry_scr[...] = jnp.zeros(carry_scr.shape, F32)
    acc_scr[...] = jnp.zeros(acc_scr.shape, F32)

    def step(kb, mask):
        start = pl.multiple_of(kb * tq, tq)
        return _sb_step(qz, _pair_blocks(kb_ref, start, tq), _pair_blocks(vb_ref, start, tq),
                        tri, carry_scr, acc_scr, mask)

    live = step(j, diag)
    lax.while_loop(lambda st: (st[0] >= 0) & (st[1] > 0),
                   lambda st: (st[0] - 1, step(st[0], None)), (j - 1, live))
    o_ref[0] = _sb_finish(acc_scr, gt_ref[0])


def _sb_prompt(qb, gtb, kbb, vbb, batch, seq):
    tq = min(SB_TQ, seq)
    assert seq % tq == 0
    r3 = lambda a: a.reshape(batch, seq, a.shape[-1])
    qblk = pl.BlockSpec((1, tq, WIDTH), lambda b, j: (b, j, 0))
    full = pl.BlockSpec((1, seq, WIDTH), lambda b, j: (b, 0, 0))
    out = pl.pallas_call(
        functools.partial(_sb_prompt_kernel, tq=tq),
        grid=(batch, seq // tq),
        in_specs=[qblk, qblk, full, full, pl.BlockSpec((2 * tq, tq), lambda b, j: (0, 0))],
        out_specs=qblk,
        out_shape=jax.ShapeDtypeStruct((batch, seq, WIDTH), BF16),
        scratch_shapes=_sb_scratch(tq),
        compiler_params=_params("parallel", "arbitrary"),
    )(r3(qb), r3(gtb), r3(kbb), r3(vbb), _sb_tri(tq))
    return out.reshape(batch * seq, WIDTH)


def _sb_sample_kernel(qb_ref, gt_ref, kb_ref, vb_ref, ckb_ref, cvb_ref, tri_ref, o_ref,
                      carry_scr, acc_scr, *, tq, kblk, past):
    ncache = past // kblk
    qz = _masked_heads(qb_ref[0])
    tri = tri_ref[...]
    diag = (lax.broadcasted_iota(I32, (tq, kblk), 1) < lax.broadcasted_iota(I32, (tq, kblk), 0))
    carry_scr[...] = jnp.zeros(carry_scr.shape, F32)
    acc_scr[...] = jnp.zeros(acc_scr.shape, F32)
    live = _sb_step(qz, [_pad_rows(t, kblk) for t in _lane_tiles(kb_ref[0])],
                    [_pad_rows(t, kblk) for t in _lane_tiles(vb_ref[0])],
                    tri, carry_scr, acc_scr, diag)

    def step(kb):
        start = pl.multiple_of(kb * kblk, kblk)
        return _sb_step(qz, _pair_blocks(ckb_ref, start, kblk, cast=True),
                        _pair_blocks(cvb_ref, start, kblk, cast=True), tri, carry_scr, acc_scr, None)

    lax.while_loop(lambda st: (st[0] >= 0) & (st[1] > 0),
                   lambda st: (st[0] - 1, step(st[0])), (jnp.int32(ncache - 1), live))
    o_ref[0] = _sb_finish(acc_scr, gt_ref[0])


def _sb_sample(qb, gtb, kbb, vbb, cache_k, cache_v, batch, tq, past):
    kblk = min(KEY_BLOCK, past)
    assert past % kblk == 0 and tq <= kblk
    r3 = lambda a: a.reshape(batch, tq, a.shape[-1])
    new = pl.BlockSpec((1, tq, WIDTH), lambda b: (b, 0, 0))
    cache = pl.BlockSpec((1, past, WIDTH), lambda b: (b, 0, 0))
    out = pl.pallas_call(
        functools.partial(_sb_sample_kernel, tq=tq, kblk=kblk, past=past),
        grid=(batch,),
        in_specs=[new, new, new, new, cache, cache, pl.BlockSpec((2 * kblk, kblk), lambda b: (0, 0))],
        out_specs=new,
        out_shape=jax.ShapeDtypeStruct((batch, tq, WIDTH), BF16),
        scratch_shapes=_sb_scratch(tq),
        compiler_params=_params("parallel"),
    )(r3(qb), r3(gtb), r3(kbb), r3(vbb), cache_k.reshape(batch, past, WIDTH),
      cache_v.reshape(batch, past, WIDTH), _sb_tri(kblk))
    return out.reshape(batch * tq, WIDTH)


def _merge_kernel(x_ref, ta_ref, tb_ref, sga_ref, sgb_ref, wa_ref, wb_ref, wo_ref, y_ref):
    ya = jnp.dot(ta_ref[...], wa_ref[...], preferred_element_type=F32)
    yb = jnp.dot(tb_ref[...], wb_ref[...], preferred_element_type=F32)
    mixed = sga_ref[...].astype(F32) * ya + sgb_ref[...].astype(F32) * yb
    y_ref[...] = x_ref[...] + jnp.dot(mixed.astype(BF16), wo_ref[...], preferred_element_type=F32)


def _merge(x2d, ta, tb, sga, sgb, wa, wb, wo):
    n, d_model = x2d.shape
    rb = min(ROW_BLOCK, n)
    row = lambda w: pl.BlockSpec((rb, w), lambda i: (i, 0))
    const = lambda a: pl.BlockSpec(a.shape, lambda i: (0, 0))
    return pl.pallas_call(
        _merge_kernel,
        grid=(n // rb,),
        in_specs=[row(d_model), row(WIDTH), row(WIDTH), row(d_model), row(d_model),
                  const(wa), const(wb), const(wo)],
        out_specs=row(d_model),
        out_shape=jax.ShapeDtypeStruct((n, d_model), F32),
        compiler_params=_params("parallel"),
    )(x2d, ta, tb, sga, sgb, wa, wb, wo)


def _layer(x, pos, past, params):
    norm_g, w_pack, q_norm_g, k_norm_g, idx_k_norm_g, wa, wb, wo = params
    b, t, d_model = x.shape
    x2d = x.reshape(b * t, d_model)
    (qa, ka, kab, va, vab, gta, qi, ki, kib, wi, qb, kb, kbb, vb, vbb, gtb, sga, sgb, vat) = _project(
        x2d, pos, t, norm_g, w_pack, q_norm_g, k_norm_g, idx_k_norm_g)
    if past is None:
        ta = _dsa_prompt(qa, qi, wi, gta, kab, vat, kib, b, t)
        tb = _sb_prompt(qb, gtb, kbb, vbb, b, t)
    else:
        c_ak, c_av, c_ik, c_bk, c_bv = past
        p_len = c_ak.shape[1]
        ta = _dsa_sample(qa, qi, wi, gta, kab, vab, kib, c_ak, c_av, c_ik, b, t, p_len)
        tb = _sb_sample(qb, gtb, kbb, vbb, c_bk, c_bv, b, t, p_len)
    y = _merge(x2d, ta, tb, sga, sgb, wa, wb, wo).reshape(b, t, d_model)
    rows = (ka.reshape(b, t, N_HEADS, HEAD_DIM), va.reshape(b, t, N_HEADS, HEAD_DIM),
            ki.reshape(b, t, IDX_DIM), kb.reshape(b, t, N_HEADS, HEAD_DIM),
            vb.reshape(b, t, N_HEADS, HEAD_DIM))
    return y, rows


def kernel(x_prompt, x_sample, cache_a_k, cache_a_v, cache_idx_k, cache_b_k, cache_b_v,
           norm_g, w_in, q_norm_g, k_norm_g, idx_k_norm_g, w_a_out, w_b_out, w_o):
    depth = norm_g.shape[0]
    d_model = x_prompt.shape[2]
    seq = x_prompt.shape[1]
    past_len = cache_a_k.shape[2]
    dec_seq = x_sample.shape[1]
    pos_p = jnp.arange(seq, dtype=I32)
    pos_s = past_len + jnp.arange(dec_seq, dtype=I32)
    yp, ys = x_prompt, x_sample
    new_p, new_s = [], []
    for l in range(depth):
        params = (norm_g[l], _pack_w_in(w_in[l], d_model), q_norm_g[l], k_norm_g[l], idx_k_norm_g[l],
                  w_a_out[l].astype(BF16), w_b_out[l].astype(BF16), w_o[l].astype(BF16))
        yp, rows_p = _layer(yp, pos_p, None, params)
        past = (cache_a_k[l], cache_a_v[l], cache_idx_k[l], cache_b_k[l], cache_b_v[l])
        ys, rows_s = _layer(ys, pos_s, past, params)
        new_p.append(rows_p)
        new_s.append(rows_s)
    stk = lambda rows, i: jnp.stack([r[i] for r in rows], axis=0)
    return (yp, ys,
            stk(new_p, 0), stk(new_p, 1), stk(new_p, 2), stk(new_p, 3), stk(new_p, 4),
            stk(new_s, 0), stk(new_s, 1), stk(new_s, 2), stk(new_s, 3), stk(new_s, 4))
```

```python
import functools

import numpy as np
import jax
import jax.numpy as jnp
from jax import lax
from jax.experimental import pallas as pl
from jax.experimental.pallas import tpu as pltpu

F32 = jnp.float32
BF16 = jnp.bfloat16
I32 = jnp.int32
I16 = jnp.int16

HEAD_DIM = 64
N_HEADS = 8
WIDTH = N_HEADS * HEAD_DIM
IDX_DIM = 64
CHUNK = 64
TOPK_MAX = 256
ROPE_THETA = 500000.0
EPS = 1e-6
NEG_INF = -1e30

LANES = 128
SUBLANES = 8
PACKED_ROWS = 16
HALF_BIAS = 1 << 15
PAIRS = WIDTH // LANES
VMEM_LIMIT_BYTES = 56 * 1024 * 1024

INT_MIN = -(2 ** 31)
INT_MAX = 2 ** 31 - 1
_NEG_BITS = int(np.array(NEG_INF, np.float32).view(np.int32))
NEG_KEY = _NEG_BITS ^ 0x7FFFFFFF

ROW_BLOCK = 256
KEY_BLOCK = 256
SB_TQ = 256
SCORE_ROWS = 128

_NT = (((1,), (1,)), ((), ()))


def _params(*sem):
    return pltpu.CompilerParams(dimension_semantics=sem, vmem_limit_bytes=VMEM_LIMIT_BYTES)


_G_QA, _G_KA, _G_VA, _G_UA, _G_QI = (i * WIDTH for i in range(5))
_G_KI = 5 * WIDTH
_G_WI = _G_KI + LANES
_G_QB = _G_WI + LANES
_G_KB, _G_VB, _G_UB = (_G_QB + i * WIDTH for i in range(1, 4))
_G_GA = _G_QB + 4 * WIDTH


def _rope(x, cs, s1, s2):
    outs = []
    for j in range(x.shape[1] // LANES):
        xj = x[:, j * LANES:(j + 1) * LANES]
        outs.append(xj * cs + pltpu.roll(xj, LANES - 8, 1) * s1 + pltpu.roll(xj, 8, 1) * s2)
    return outs[0] if len(outs) == 1 else jnp.concatenate(outs, axis=1)


def _proj_kernel(x_ref, g_ref, w_ref, cs_ref, s1_ref, s2_ref, qg_ref, kg_ref, ig_ref, bd_ref,
                 qa_o, ka_o, kab_o, va_o, vab_o, gta_o, qi_o, ki_o, kib_o, wi_o,
                 qb_o, kb_o, kbb_o, vb_o, vbb_o, gtb_o, sga_o, sgb_o, vat_o, *, d_model, rows_t):
    x = x_ref[...]
    ms = jnp.mean(x * x, axis=-1, keepdims=True)
    xn = (x * lax.rsqrt(ms + EPS) * g_ref[...]).astype(BF16)
    cs, s1, s2 = cs_ref[...], s1_ref[...], s2_ref[...]

    def proj(c0, width):
        return jnp.dot(xn, w_ref[:, c0:c0 + width], preferred_element_type=F32)

    def put_row(o_ref, val):
        if rows_t:
            o_ref[0] = val.T
        else:
            o_ref[...] = val

    def head_norm(h, gain):
        ss = jnp.dot((h * h).astype(BF16), bd_ref[...], preferred_element_type=F32)
        return h * lax.rsqrt(ss * (1.0 / HEAD_DIM) + EPS) * gain

    qa = _rope(head_norm(proj(_G_QA, WIDTH), qg_ref[...]), cs, s1, s2)
    qa_o[...] = (qa * (HEAD_DIM ** -0.5)).astype(BF16)
    ka = _rope(head_norm(proj(_G_KA, WIDTH), kg_ref[...]), cs, s1, s2)
    put_row(ka_o, ka)
    kab_o[...] = ka.astype(BF16)
    va = proj(_G_VA, WIDTH)
    put_row(va_o, va)
    vab_o[...] = va.astype(BF16)
    vat_o[0] = va.T.astype(BF16)
    ua = proj(_G_UA, WIDTH)
    gta_o[...] = (ua * jax.nn.sigmoid(ua)).astype(BF16)
    qi_o[...] = _rope(proj(_G_QI, WIDTH), cs, s1, s2).astype(BF16)

    kw = proj(_G_KI, 2 * LANES)
    hk = kw[:, :LANES]
    ssk = jnp.sum(hk * hk, axis=-1, keepdims=True)
    kin = hk * lax.rsqrt(ssk * (1.0 / IDX_DIM) + EPS) * ig_ref[...]
    ki_pad = _rope(kin, cs, s1, s2)
    ki = ki_pad[:, :IDX_DIM]
    if rows_t:
        ki_o[0] = ki_pad.T[:IDX_DIM]
    else:
        ki_o[...] = ki
    kib_o[...] = ki.astype(BF16)
    wi_o[...] = kw[:, LANES:] * (N_HEADS ** -0.5) * (IDX_DIM ** -0.5)

    qb_o[...] = (proj(_G_QB, WIDTH) * (HEAD_DIM ** -0.5)).astype(BF16)
    kb = proj(_G_KB, WIDTH)
    put_row(kb_o, kb)
    kbb_o[...] = kb.astype(BF16)
    vb = proj(_G_VB, WIDTH)
    put_row(vb_o, vb)
    vbb_o[...] = vb.astype(BF16)
    ub = proj(_G_UB, WIDTH)
    gtb_o[...] = (ub * jax.nn.sigmoid(ub)).astype(BF16)
    sga_o[...] = jax.nn.sigmoid(proj(_G_GA, d_model)).astype(BF16)
    sgb_o[...] = jax.nn.sigmoid(proj(_G_GA + d_model, d_model)).astype(BF16)


def _pack_w_in(w_in, d_model):
    o = 0
    parts = []

    def take(n, pad_to=None):
        nonlocal o
        blk = w_in[:, o:o + n]
        o += n
        if pad_to is not None and pad_to > n:
            blk = jnp.pad(blk, ((0, 0), (0, pad_to - n)))
        parts.append(blk)

    for _ in range(5):
        take(WIDTH)
    take(IDX_DIM, LANES)
    take(N_HEADS, LANES)
    for _ in range(4):
        take(WIDTH)
    take(d_model)
    take(d_model)
    assert o == w_in.shape[1]
    return jnp.concatenate(parts, axis=1).astype(BF16)


def _rope_tables(pos):
    half = HEAD_DIM // 8
    inv_freq = ROPE_THETA ** (-jnp.arange(half, dtype=F32) / half)
    ang = pos.astype(F32)[:, None] * inv_freq[None, :]
    cos, sin = jnp.cos(ang), jnp.sin(ang)
    t = pos.shape[0]
    ones = jnp.ones((t, HEAD_DIM - 2 * half), F32)
    zeros = jnp.zeros((t, HEAD_DIM - 2 * half), F32)
    z8 = jnp.zeros((t, half), F32)
    cs = jnp.concatenate([cos, cos, ones], axis=1)
    s1 = jnp.concatenate([-sin, z8, zeros], axis=1)
    s2 = jnp.concatenate([z8, sin, zeros], axis=1)
    rep = lambda a: jnp.concatenate([a, a], axis=1)
    return rep(cs), rep(s1), rep(s2)


def _project(x2d, pos, t_len, norm_g, w_pack, q_norm_g, k_norm_g, idx_k_norm_g, rows_t):
    n, d_model = x2d.shape
    rb = min(ROW_BLOCK, n)
    assert n % rb == 0
    cs, s1, s2 = _rope_tables(pos)
    if t_len >= rb:
        assert t_len % rb == 0
        tb = t_len // rb
        tab_map = lambda i: (i % tb, 0)
    else:
        assert rb % t_len == 0
        cs, s1, s2 = (jnp.tile(a, (rb // t_len, 1)) for a in (cs, s1, s2))
        tab_map = lambda i: (0, 0)
    qg = jnp.tile(q_norm_g, N_HEADS)[None, :]
    kg = jnp.tile(k_norm_g, N_HEADS)[None, :]
    ig = jnp.pad(idx_k_norm_g, (0, LANES - IDX_DIM))[None, :]
    hid = np.arange(WIDTH) // HEAD_DIM
    bd = jnp.asarray(hid[:, None] == hid[None, :], BF16)
    e_pack = w_pack.shape[1]

    row = lambda w: pl.BlockSpec((rb, w), lambda i: (i, 0))
    const = lambda shape: pl.BlockSpec(shape, lambda i: (0, 0))
    f32o = lambda w: jax.ShapeDtypeStruct((n, w), F32)
    if rows_t:
        assert t_len % rb == 0
        tb = t_len // rb
        rowo = lambda w: (jax.ShapeDtypeStruct((n // t_len, w, t_len), F32),
                          pl.BlockSpec((1, w, rb), lambda i: (i // tb, 0, i % tb)))
    else:
        rowo = lambda w: (f32o(w), row(w))
    b16o = lambda w: jax.ShapeDtypeStruct((n, w), BF16)
    outs = [
        (b16o(WIDTH), row(WIDTH)),
        rowo(WIDTH),
        (b16o(WIDTH), row(WIDTH)),
        rowo(WIDTH),
        (b16o(WIDTH), row(WIDTH)),
        (b16o(WIDTH), row(WIDTH)),
        (b16o(WIDTH), row(WIDTH)),
        rowo(IDX_DIM),
        (b16o(IDX_DIM), row(IDX_DIM)),
        (f32o(LANES), row(LANES)),
        (b16o(WIDTH), row(WIDTH)),
        rowo(WIDTH),
        (b16o(WIDTH), row(WIDTH)),
        rowo(WIDTH),
        (b16o(WIDTH), row(WIDTH)),
        (b16o(WIDTH), row(WIDTH)),
        (b16o(d_model), row(d_model)),
        (b16o(d_model), row(d_model)),
        (jax.ShapeDtypeStruct((n // rb, WIDTH, rb), BF16),
         pl.BlockSpec((1, WIDTH, rb), lambda i: (i, 0, 0))),
    ]
    return pl.pallas_call(
        functools.partial(_proj_kernel, d_model=d_model, rows_t=rows_t),
        grid=(n // rb,),
        in_specs=[row(d_model), const((1, d_model)), const((d_model, e_pack)),
                  pl.BlockSpec((rb, LANES), tab_map), pl.BlockSpec((rb, LANES), tab_map),
                  pl.BlockSpec((rb, LANES), tab_map),
                  const((1, WIDTH)), const((1, WIDTH)), const((1, LANES)), const((WIDTH, WIDTH))],
        out_specs=[o[1] for o in outs],
        out_shape=[o[0] for o in outs],
        compiler_params=_params("parallel"),
    )(x2d, norm_g[None, :], w_pack, cs, s1, s2, qg, kg, ig, bd)


def _sort_key(score):
    b = lax.bitcast_convert_type(score, I32)
    b = jnp.where(b == INT_MIN, 0, b)
    return jnp.where(b < 0, b ^ INT_MAX, b)


def _lane_tiles(x):
    return [x[:, c * LANES:(c + 1) * LANES] for c in range(x.shape[1] // LANES)]


def _chunk_limit(pos):
    shift = CHUNK.bit_length() - 1
    assert CHUNK == 1 << shift
    return ((pos >> shift) + 1) << shift


def _head_masks():
    lane = lax.broadcasted_iota(I32, (1, LANES), 1)
    return lane < HEAD_DIM


def _masked_heads(q):
    lo = _head_masks()
    out = []
    for qp in _lane_tiles(q):
        zero = jnp.zeros_like(qp)
        out += [jnp.where(lo, qp, zero), jnp.where(lo, zero, qp)]
    return out


def _rows8(x):
    return x.reshape(x.shape[0] // SUBLANES, SUBLANES, x.shape[1])


def _fold_rows(x, op):
    return op(_rows8(x), axis=0)


def _index_queries(qi, wip):
    q_heads = [qi[:, h * IDX_DIM:(h + 1) * IDX_DIM] for h in range(N_HEADS)]
    wt = wip.T
    tq = qi.shape[0]
    return q_heads, [jnp.broadcast_to(wt[h:h + 1, :], (SUBLANES, tq)) for h in range(N_HEADS)]


def _block_scores(ki_blk, q_heads, w_rows):
    parts = []
    for r0 in range(0, ki_blk.shape[0], SCORE_ROWS):
        sc = None
        for h in range(N_HEADS):
            d = lax.dot_general(ki_blk[r0:r0 + SCORE_ROWS], q_heads[h], _NT,
                                preferred_element_type=F32)
            t = jnp.maximum(_rows8(d), 0.0) * w_rows[h][None]
            sc = t if sc is None else sc + t
        parts.append(sc.reshape(SCORE_ROWS, sc.shape[2]))
    return jnp.concatenate(parts, axis=0)


def _select_bias(keys_scr, half_scr, bias_scr, nkb, limit, topk, n_unprocessed):
    _, kblk, tq = keys_scr.shape

    def count(pred):
        def body(kb, acc):
            return acc + _fold_rows(jnp.where(pred(keys_scr[kb], kb), 1, 0), jnp.sum)
        acc = lax.fori_loop(0, nkb, body, jnp.zeros((SUBLANES, tq), I32))
        return jnp.sum(acc, axis=0, keepdims=True)

    def count_ge(cand):
        c = count(lambda k, kb: k >= cand)
        return c + jnp.where(cand <= NEG_KEY, n_unprocessed, 0)

    hi_scr, lo_scr = half_scr.at[0], half_scr.at[1]
    neg_hi, neg_lo = NEG_KEY >> 16, (NEG_KEY & 0xFFFF) - HALF_BIAS

    def split(kb, c):
        k = keys_scr[kb]
        hi_scr[kb] = (k >> 16).astype(I16)
        lo_scr[kb] = ((k & 0xFFFF) - HALF_BIAS).astype(I16)
        return c

    lax.fori_loop(0, nkb, split, 0)

    def count16(scr, cand):
        c16 = cand.astype(I16)

        def body(kb, acc):
            hit = jnp.where(scr[kb] >= c16, jnp.int16(1), jnp.int16(0))
            tiles = [hit[r:r + PACKED_ROWS] for r in range(0, kblk, PACKED_ROWS)]
            return acc + functools.reduce(jnp.add, tiles)
        acc = lax.fori_loop(0, nkb, body, jnp.zeros((PACKED_ROWS, tq), I16))
        return jnp.sum(acc.astype(I32), axis=0, keepdims=True)

    def search16(scr, target, extra):
        def step(i, t):
            cand = t + lax.shift_left(jnp.int32(1), 15 - i)
            return jnp.where(count16(scr, cand) + extra(cand) >= target, cand, t)
        return lax.fori_loop(0, 16, step, jnp.full((1, tq), -HALF_BIAS, I32))

    hi_extra = lambda cand: jnp.where(cand <= neg_hi, n_unprocessed, 0)
    thr_hi = search16(hi_scr, topk, hi_extra)
    above = count16(hi_scr, thr_hi + 1) + hi_extra(thr_hi + 1)
    thr_hi16 = thr_hi.astype(I16)

    def keep_matching(kb, c):
        lo_scr[kb] = jnp.where(hi_scr[kb] == thr_hi16, lo_scr[kb], jnp.int16(-HALF_BIAS))
        return c

    lax.fori_loop(0, nkb, keep_matching, 0)
    lo_extra = lambda cand: jnp.where((thr_hi == neg_hi) & (cand <= neg_lo), n_unprocessed, 0)
    thr_lo = search16(lo_scr, topk - above, lo_extra)
    thr = lax.shift_left(thr_hi, 16) + (thr_lo + HALF_BIAS)
    c_gt = count_ge(thr + 1)
    c_ge = count_ge(thr)
    need = topk - c_gt
    excess = jnp.max(jnp.where(c_ge - c_gt > need, 1, 0)) > 0

    def key_index(kb):
        return kb * kblk + lax.broadcasted_iota(I32, (kblk, tq), 0)

    nbits = int(keys_scr.shape[0] * kblk).bit_length() + 1

    def tie_cut():
        def step(i, cut):
            cand = cut + lax.shift_left(jnp.int32(1), nbits - 1 - i)
            c = count(lambda k, kb: (k == thr) & (key_index(kb) < cand))
            return jnp.where(c <= need, cand, cut)
        return lax.fori_loop(0, nbits, step, jnp.zeros((1, tq), I32))

    cut = lax.cond(excess, tie_cut, lambda: jnp.full((1, tq), INT_MAX, I32))

    def write(kb, c):
        k = keys_scr[kb]
        idx = key_index(kb)
        sel = ((k > thr) | ((k == thr) & (idx < cut))) & (idx < limit)
        bias_scr[kb] = jnp.where(sel, 0.0, NEG_INF)
        return c

    lax.fori_loop(0, nkb, write, 0)


def _attention(qa, nkb, load, bias_scr, lg_scr, m_scr, mn_scr, l_scr, acc_scr):
    qz = _masked_heads(qa)
    m_scr[...] = jnp.full(m_scr.shape, NEG_INF, F32)
    l_scr[...] = jnp.zeros(l_scr.shape, F32)
    acc_scr[...] = jnp.zeros(acc_scr.shape, F32)

    def step(kb, k_pairs, vt):
        bias = bias_scr[kb]
        for h in range(N_HEADS):
            lg = lax.dot_general(k_pairs[h // 2], qz[h], _NT, preferred_element_type=F32) + bias
            lg_scr[h] = lg
            blk_max = jnp.max(_fold_rows(lg, jnp.max), axis=0, keepdims=True)
            mn_scr[h] = jnp.maximum(m_scr[h], blk_max)
        for h in range(N_HEADS):
            m_new = mn_scr[h]
            alpha = jnp.exp(m_scr[h] - m_new)
            p = jnp.exp(_rows8(lg_scr[h]) - m_new[None])
            l_scr[h] = alpha * l_scr[h] + jnp.sum(jnp.sum(p, axis=0), axis=0, keepdims=True)
            pv = jnp.dot(vt[h * HEAD_DIM:(h + 1) * HEAD_DIM, :],
                         p.reshape(lg_scr.shape[1:]).astype(BF16),
                         preferred_element_type=F32)
            acc_scr[h] = (_rows8(acc_scr[h]) * alpha[None]).reshape(pv.shape) + pv
            m_scr[h] = m_new

    def body(kb, c):
        step(kb, *load(kb))
        return c

    lax.fori_loop(0, nkb, body, 0)
    outs = [(_rows8(acc_scr[h]) / l_scr[h][None]).reshape(acc_scr.shape[1:]) for h in range(N_HEADS)]
    return jnp.concatenate(outs, axis=0).T


def _attention_scratch(nkb, tq, kblk):
    return [pltpu.VMEM((nkb, kblk, tq), I32),
            pltpu.VMEM((2, nkb, kblk, tq), I16),
            pltpu.VMEM((nkb, kblk, tq), F32),
            pltpu.VMEM((N_HEADS, kblk, tq), F32),
            pltpu.VMEM((N_HEADS, SUBLANES, tq), F32),
            pltpu.VMEM((N_HEADS, SUBLANES, tq), F32),
            pltpu.VMEM((N_HEADS, SUBLANES, tq), F32),
            pltpu.VMEM((N_HEADS, HEAD_DIM, tq), F32)]


def _pair_blocks(ref, start, rows):
    return [ref[0, pl.ds(start, rows), p * LANES:(p + 1) * LANES] for p in range(PAIRS)]


def _dsa_prompt_kernel(qa_ref, qi_ref, wi_ref, gt_ref, ka_ref, vt_ref, ki_ref, o_ref,
                       keys_scr, half_scr, bias_scr, lg_scr, m_scr, mn_scr, l_scr, acc_scr, *, tq, topk, seq):
    kblk = tq
    j = pl.program_id(1)
    nkb = j + 1
    pos = j * tq + lax.broadcasted_iota(I32, (1, tq), 1)
    limit = _chunk_limit(pos)
    q_heads, w_rows = _index_queries(qi_ref[0], wi_ref[0])

    def score_body(kb, c):
        start = pl.multiple_of(kb * kblk, kblk)
        sc = _block_scores(ki_ref[0, pl.ds(start, kblk), :], q_heads, w_rows)
        kpos = kb * kblk + lax.broadcasted_iota(I32, (kblk, tq), 0)
        keys_scr[kb] = _sort_key(jnp.where(kpos < limit, sc, NEG_INF))
        return c

    lax.fori_loop(0, nkb, score_body, 0)
    _select_bias(keys_scr, half_scr, bias_scr,nkb, limit, topk, seq - nkb * kblk)

    def load(kb):
        return _pair_blocks(ka_ref, pl.multiple_of(kb * kblk, kblk), kblk), vt_ref[0, kb]

    oa = _attention(qa_ref[0], nkb, load, bias_scr, lg_scr, m_scr, mn_scr, l_scr, acc_scr)
    o_ref[0] = (oa * gt_ref[0].astype(F32)).astype(BF16)


def _dsa_prompt(qa, qi, wip, gta, kab, vat, kib, batch, seq):
    tq = ROW_BLOCK
    assert seq % tq == 0 and tq % CHUNK == 0
    topk = min(TOPK_MAX, seq // 4)
    nblk = seq // tq
    r3 = lambda a: a.reshape(batch, seq, a.shape[-1])
    qblk = lambda w: pl.BlockSpec((1, tq, w), lambda b, j: (b, j, 0))
    full = lambda w: pl.BlockSpec((1, seq, w), lambda b, j: (b, 0, 0))
    out = pl.pallas_call(
        functools.partial(_dsa_prompt_kernel, tq=tq, topk=topk, seq=seq),
        grid=(batch, nblk),
        in_specs=[qblk(WIDTH), qblk(WIDTH), qblk(LANES), qblk(WIDTH), full(WIDTH),
                  pl.BlockSpec((1, nblk, WIDTH, tq), lambda b, j: (b, 0, 0, 0)), full(IDX_DIM)],
        out_specs=qblk(WIDTH),
        out_shape=jax.ShapeDtypeStruct((batch, seq, WIDTH), BF16),
        scratch_shapes=_attention_scratch(nblk, tq, tq),
        compiler_params=_params("parallel", "arbitrary"),
    )(r3(qa), r3(qi), r3(wip), r3(gta), r3(kab), vat.reshape(batch, nblk, WIDTH, tq), r3(kib))
    return out.reshape(batch * seq, WIDTH)


def _pad_rows(x, rows):
    return jnp.concatenate([x, jnp.zeros((rows - x.shape[0], x.shape[1]), x.dtype)], axis=0)


def _dsa_sample_kernel(qa_ref, qi_ref, wi_ref, gt_ref, ka_ref, va_ref, ki_ref,
                       ckt_ref, cvt_ref, ckit_ref, o_ref, keys_scr, bias_scr, *, tq, topk, past):
    total = past + tq
    width = past + LANES
    pos = past + lax.broadcasted_iota(I32, (tq, 1), 0)
    limit = jnp.minimum(_chunk_limit(pos), total)
    kpos = lax.broadcasted_iota(I32, (tq, width), 1)

    qi, wi = qi_ref[0], wi_ref[0]
    q_all = jnp.concatenate([qi[:, h * IDX_DIM:(h + 1) * IDX_DIM] for h in range(N_HEADS)], axis=0)
    w_all = jnp.concatenate([jnp.broadcast_to(wi[:, h:h + 1], (tq, LANES)) for h in range(N_HEADS)],
                            axis=0)

    def head_sum(d):
        t = jnp.maximum(d, 0.0) * w_all
        return functools.reduce(jnp.add, [t[h * tq:(h + 1) * tq] for h in range(N_HEADS)])

    kit = ckit_ref[0].astype(BF16)
    tiles = [head_sum(jnp.dot(q_all, kit[:, c:c + LANES], preferred_element_type=F32))
             for c in range(0, past, LANES)]
    tiles.append(head_sum(lax.dot_general(q_all, _pad_rows(ki_ref[0], LANES), _NT,
                                          preferred_element_type=F32)))
    keys = _sort_key(jnp.where(kpos < limit, jnp.concatenate(tiles, axis=1), NEG_INF))
    keys_scr[...] = jnp.where(kpos < total, keys, INT_MIN)

    def count(pred):
        hit = jnp.where(pred(keys_scr[...]), 1, 0)
        return jnp.sum(functools.reduce(jnp.add, _lane_tiles(hit)), axis=1, keepdims=True)

    def search(i, thr):
        cand = thr + lax.shift_left(jnp.int32(1), 31 - i)
        return jnp.where(count(lambda k: k >= cand) >= topk, cand, thr)

    thr = lax.fori_loop(0, 32, search, jnp.full((tq, 1), INT_MIN, I32))
    c_gt = count(lambda k: k > thr)
    c_eq = count(lambda k: k == thr)
    need = topk - c_gt
    nbits = int(width).bit_length() + 1

    def tie_cut():
        def step(i, cut):
            cand = cut + lax.shift_left(jnp.int32(1), nbits - 1 - i)
            c = count(lambda k: (k == thr) & (kpos < cand))
            return jnp.where(c <= need, cand, cut)
        return lax.fori_loop(0, nbits, step, jnp.zeros((tq, 1), I32))

    excess = jnp.max(jnp.where(c_eq > need, 1, 0)) > 0
    cut = lax.cond(excess, tie_cut, lambda: jnp.full((tq, 1), INT_MAX, I32))
    k = keys_scr[...]
    sel = ((k > thr) | ((k == thr) & (kpos < cut))) & (kpos < limit)
    bias_scr[...] = jnp.where(sel, 0.0, NEG_INF)

    qa, ka_new, va_new = qa_ref[0], ka_ref[0], va_ref[0]
    outs = []
    for h in range(N_HEADS):
        rows = slice(h * HEAD_DIM, (h + 1) * HEAD_DIM)
        q_h = qa[:, rows]
        k_new = _pad_rows(ka_new[:, rows], LANES)
        v_new = _pad_rows(va_new[:, rows], LANES)
        lg_c = jnp.dot(q_h, ckt_ref[0, rows, :].astype(BF16), preferred_element_type=F32)
        lg_c = lg_c + bias_scr[:, :past]
        lg_n = lax.dot_general(q_h, k_new, _NT, preferred_element_type=F32) + bias_scr[:, past:]
        m = jnp.maximum(jnp.max(lg_c, axis=1, keepdims=True), jnp.max(lg_n, axis=1, keepdims=True))
        p_c = jnp.exp(lg_c - m)
        p_n = jnp.exp(lg_n - m)
        l = jnp.sum(p_c, axis=1, keepdims=True) + jnp.sum(p_n, axis=1, keepdims=True)
        o = lax.dot_general(p_c.astype(BF16), cvt_ref[0, rows, :].astype(BF16), _NT,
                            preferred_element_type=F32)
        o = o + jnp.dot(p_n.astype(BF16), v_new, preferred_element_type=F32)
        outs.append(o / l)
    oa = jnp.concatenate(outs, axis=1)
    o_ref[0] = (oa * gt_ref[0].astype(F32)).astype(BF16)


def _dsa_sample(qa, qi, wip, gta, kab, vab, kib, cache_kt, cache_vt, cache_kit, batch, tq, past):
    assert past % LANES == 0 and tq <= LANES
    topk = min(TOPK_MAX, (past + tq) // 4)
    assert past + tq >= topk
    r3 = lambda a: a.reshape(batch, tq, a.shape[-1])
    new = lambda w: pl.BlockSpec((1, tq, w), lambda b: (b, 0, 0))
    cache = lambda rows: pl.BlockSpec((1, rows, past), lambda b: (b, 0, 0))
    out = pl.pallas_call(
        functools.partial(_dsa_sample_kernel, tq=tq, topk=topk, past=past),
        grid=(batch,),
        in_specs=[new(WIDTH), new(WIDTH), new(LANES), new(WIDTH),
                  new(WIDTH), new(WIDTH), new(IDX_DIM),
                  cache(WIDTH), cache(WIDTH), cache(IDX_DIM)],
        out_specs=new(WIDTH),
        out_shape=jax.ShapeDtypeStruct((batch, tq, WIDTH), BF16),
        scratch_shapes=[pltpu.VMEM((tq, past + LANES), I32), pltpu.VMEM((tq, past + LANES), F32)],
        compiler_params=_params("parallel"),
    )(r3(qa), r3(qi), r3(wip), r3(gta), r3(kab), r3(vab), r3(kib), cache_kt, cache_vt, cache_kit)
    return out.reshape(batch * tq, WIDTH)


SB_DEAD = -104.0


def _sb_step(z_of, pv_of, tri, carry_scr, acc_scr, mask):
    tq = carry_scr.shape[1]
    live = None
    for h in range(N_HEADS):
        z = z_of(h)
        lg = jnp.log(1.0 + jnp.exp(-jnp.abs(z)))
        log1m = -(jnp.maximum(z, 0.0) + lg)
        if mask is not None:
            log1m = jnp.where(mask, log1m, 0.0)
        logsig = jnp.minimum(z, 0.0) - lg
        hi = log1m.astype(BF16)
        lo = (log1m - hi.astype(F32)).astype(BF16)
        after = jnp.dot(jnp.concatenate([hi, lo], axis=1), tri, preferred_element_type=F32)
        carry = carry_scr[h]
        a = jnp.concatenate([jnp.exp(t + carry) for t in _lane_tiles(logsig + after)], axis=1)
        if mask is not None:
            a = jnp.where(mask, a, 0.0)
        acc_scr[h] += pv_of(h, a.astype(BF16))
        carry = carry + jnp.broadcast_to(jnp.sum(log1m, axis=1, keepdims=True), (tq, LANES))
        carry_scr[h] = carry
        live = carry if live is None else jnp.maximum(live, carry)
    return (jnp.max(live) > SB_DEAD).astype(I32)


def _sb_pair_fns(qz, k_pairs, v_pairs):
    z_of = lambda h: lax.dot_general(qz[h], k_pairs[h // 2], _NT, preferred_element_type=F32)
    pv_of = lambda h, a: jnp.dot(a, v_pairs[h // 2], preferred_element_type=F32)
    return z_of, pv_of


def _sb_finish(acc_scr, gt):
    lo = _head_masks()
    outs = [jnp.where(lo, acc_scr[2 * p], acc_scr[2 * p + 1]) for p in range(PAIRS)]
    return (jnp.concatenate(outs, axis=1) * gt.astype(F32)).astype(BF16)


def _sb_tri(kblk):
    jj = np.arange(kblk)
    m = (jj[:, None] > jj[None, :])
    return jnp.asarray(np.concatenate([m, m], axis=0), BF16)


def _sb_scratch(tq):
    return [pltpu.VMEM((N_HEADS, tq, LANES), F32), pltpu.VMEM((N_HEADS, tq, LANES), F32)]


def _sb_prompt_kernel(qb_ref, gt_ref, kb_ref, vb_ref, tri_ref, o_ref, carry_scr, acc_scr, *, tq):
    j = pl.program_id(1)
    qz = _masked_heads(qb_ref[0])
    tri = tri_ref[...]
    diag = (lax.broadcasted_iota(I32, (tq, tq), 1) < lax.broadcasted_iota(I32, (tq, tq), 0))
    carry_scr[...] = jnp.zeros(carry_scr.shape, F32)
    acc_scr[...] = jnp.zeros(acc_scr.shape, F32)

    def step(kb, mask):
        start = pl.multiple_of(kb * tq, tq)
        fns = _sb_pair_fns(qz, _pair_blocks(kb_ref, start, tq), _pair_blocks(vb_ref, start, tq))
        return _sb_step(*fns, tri, carry_scr, acc_scr, mask)

    live = step(j, diag)
    lax.while_loop(lambda st: (st[0] >= 0) & (st[1] > 0),
                   lambda st: (st[0] - 1, step(st[0], None)), (j - 1, live))
    o_ref[0] = _sb_finish(acc_scr, gt_ref[0])


def _sb_prompt(qb, gtb, kbb, vbb, batch, seq):
    tq = min(SB_TQ, seq)
    assert seq % tq == 0
    r3 = lambda a: a.reshape(batch, seq, a.shape[-1])
    qblk = pl.BlockSpec((1, tq, WIDTH), lambda b, j: (b, j, 0))
    full = pl.BlockSpec((1, seq, WIDTH), lambda b, j: (b, 0, 0))
    out = pl.pallas_call(
        functools.partial(_sb_prompt_kernel, tq=tq),
        grid=(batch, seq // tq),
        in_specs=[qblk, qblk, full, full, pl.BlockSpec((2 * tq, tq), lambda b, j: (0, 0))],
        out_specs=qblk,
        out_shape=jax.ShapeDtypeStruct((batch, seq, WIDTH), BF16),
        scratch_shapes=_sb_scratch(tq),
        compiler_params=_params("parallel", "arbitrary"),
    )(r3(qb), r3(gtb), r3(kbb), r3(vbb), _sb_tri(tq))
    return out.reshape(batch * seq, WIDTH)


def _sb_sample_kernel(qb_ref, gt_ref, kb_ref, vb_ref, ckt_ref, cvt_ref, tri_ref, o_ref,
                      carry_scr, acc_scr, *, tq, kblk, past):
    ncache = past // kblk
    qb = qb_ref[0]
    tri = tri_ref[...]
    diag = (lax.broadcasted_iota(I32, (tq, kblk), 1) < lax.broadcasted_iota(I32, (tq, kblk), 0))
    carry_scr[...] = jnp.zeros(carry_scr.shape, F32)
    acc_scr[...] = jnp.zeros(acc_scr.shape, F32)
    fns = _sb_pair_fns(_masked_heads(qb), [_pad_rows(t, kblk) for t in _lane_tiles(kb_ref[0])],
                       [_pad_rows(t, kblk) for t in _lane_tiles(vb_ref[0])])
    live = _sb_step(*fns, tri, carry_scr, acc_scr, diag)
    q_heads = [qb[:, h * HEAD_DIM:(h + 1) * HEAD_DIM] for h in range(N_HEADS)]
    zeros = jnp.zeros((tq, HEAD_DIM), F32)

    def cache_step(kb):
        cols = slice(kb * kblk, (kb + 1) * kblk)

        def z_of(h):
            kt = ckt_ref[0, h * HEAD_DIM:(h + 1) * HEAD_DIM, cols].astype(BF16)
            return jnp.dot(q_heads[h], kt, preferred_element_type=F32)

        def pv_of(h, a):
            vt = cvt_ref[0, h * HEAD_DIM:(h + 1) * HEAD_DIM, cols].astype(BF16)
            pv = lax.dot_general(a, vt, _NT, preferred_element_type=F32)
            return jnp.concatenate([pv, zeros] if h % 2 == 0 else [zeros, pv], axis=1)

        return _sb_step(z_of, pv_of, tri, carry_scr, acc_scr, None)

    for kb in reversed(range(ncache)):
        live = lax.cond(live > 0, functools.partial(cache_step, kb), lambda: jnp.int32(0))
    o_ref[0] = _sb_finish(acc_scr, gt_ref[0])


def _sb_sample(qb, gtb, kbb, vbb, cache_kt, cache_vt, batch, tq, past):
    kblk = min(KEY_BLOCK, past)
    assert past % kblk == 0 and tq <= kblk
    r3 = lambda a: a.reshape(batch, tq, a.shape[-1])
    new = pl.BlockSpec((1, tq, WIDTH), lambda b: (b, 0, 0))
    cache = pl.BlockSpec((1, WIDTH, past), lambda b: (b, 0, 0))
    out = pl.pallas_call(
        functools.partial(_sb_sample_kernel, tq=tq, kblk=kblk, past=past),
        grid=(batch,),
        in_specs=[new, new, new, new, cache, cache, pl.BlockSpec((2 * kblk, kblk), lambda b: (0, 0))],
        out_specs=new,
        out_shape=jax.ShapeDtypeStruct((batch, tq, WIDTH), BF16),
        scratch_shapes=_sb_scratch(tq),
        compiler_params=_params("parallel"),
    )(r3(qb), r3(gtb), r3(kbb), r3(vbb), cache_kt, cache_vt, _sb_tri(kblk))
    return out.reshape(batch * tq, WIDTH)


def _merge_kernel(x_ref, ta_ref, tb_ref, sga_ref, sgb_ref, wa_ref, wb_ref, wo_ref, y_ref):
    ya = jnp.dot(ta_ref[...], wa_ref[...], preferred_element_type=F32)
    yb = jnp.dot(tb_ref[...], wb_ref[...], preferred_element_type=F32)
    mixed = sga_ref[...].astype(F32) * ya + sgb_ref[...].astype(F32) * yb
    y_ref[...] = x_ref[...] + jnp.dot(mixed.astype(BF16), wo_ref[...], preferred_element_type=F32)


def _merge(x2d, ta, tb, sga, sgb, wa, wb, wo):
    n, d_model = x2d.shape
    rb = min(ROW_BLOCK, n)
    row = lambda w: pl.BlockSpec((rb, w), lambda i: (i, 0))
    const = lambda a: pl.BlockSpec(a.shape, lambda i: (0, 0))
    return pl.pallas_call(
        _merge_kernel,
        grid=(n // rb,),
        in_specs=[row(d_model), row(WIDTH), row(WIDTH), row(d_model), row(d_model),
                  const(wa), const(wb), const(wo)],
        out_specs=row(d_model),
        out_shape=jax.ShapeDtypeStruct((n, d_model), F32),
        compiler_params=_params("parallel"),
    )(x2d, ta, tb, sga, sgb, wa, wb, wo)


def _layer(x, pos, past, params):
    norm_g, w_pack, q_norm_g, k_norm_g, idx_k_norm_g, wa, wb, wo = params
    b, t, d_model = x.shape
    x2d = x.reshape(b * t, d_model)
    (qa, ka, kab, va, vab, gta, qi, ki, kib, wi, qb, kb, kbb, vb, vbb, gtb, sga, sgb, vat) = _project(
        x2d, pos, t, norm_g, w_pack, q_norm_g, k_norm_g, idx_k_norm_g, rows_t=past is None)
    if past is None:
        ta = _dsa_prompt(qa, qi, wi, gta, kab, vat, kib, b, t)
        tb = _sb_prompt(qb, gtb, kbb, vbb, b, t)
    else:
        p_len = past[0].shape[1]
        keys_minor = lambda c: jnp.moveaxis(c, 1, -1).reshape(b, -1, p_len)
        c_ak, c_av, c_ik, c_bk, c_bv = (keys_minor(c) for c in past)
        ta = _dsa_sample(qa, qi, wi, gta, kab, vab, kib, c_ak, c_av, c_ik, b, t, p_len)
        tb = _sb_sample(qb, gtb, kbb, vbb, c_bk, c_bv, b, t, p_len)
    y = _merge(x2d, ta, tb, sga, sgb, wa, wb, wo).reshape(b, t, d_model)
    if past is None:
        heads = lambda a: a.reshape(b, N_HEADS, HEAD_DIM, t).transpose(0, 3, 1, 2)
        rows = (heads(ka), heads(va), ki.transpose(0, 2, 1), heads(kb), heads(vb))
    else:
        heads = lambda a: a.reshape(b, t, N_HEADS, HEAD_DIM)
        rows = (heads(ka), heads(va), ki.reshape(b, t, IDX_DIM), heads(kb), heads(vb))
    return y, rows


def kernel(x_prompt, x_sample, cache_a_k, cache_a_v, cache_idx_k, cache_b_k, cache_b_v,
           norm_g, w_in, q_norm_g, k_norm_g, idx_k_norm_g, w_a_out, w_b_out, w_o):
    depth = norm_g.shape[0]
    d_model = x_prompt.shape[2]
    seq = x_prompt.shape[1]
    past_len = cache_a_k.shape[2]
    dec_seq = x_sample.shape[1]
    pos_p = jnp.arange(seq, dtype=I32)
    pos_s = past_len + jnp.arange(dec_seq, dtype=I32)
    yp, ys = x_prompt, x_sample
    new_p, new_s = [], []
    for l in range(depth):
        params = (norm_g[l], _pack_w_in(w_in[l], d_model), q_norm_g[l], k_norm_g[l], idx_k_norm_g[l],
                  w_a_out[l].astype(BF16), w_b_out[l].astype(BF16), w_o[l].astype(BF16))
        yp, rows_p = _layer(yp, pos_p, None, params)
        past = (cache_a_k[l], cache_a_v[l], cache_idx_k[l], cache_b_k[l], cache_b_v[l])
        ys, rows_s = _layer(ys, pos_s, past, params)
        new_p.append(rows_p)
        new_s.append(rows_s)
    stk = lambda rows, i: jnp.stack([r[i] for r in rows], axis=0)
    return (yp, ys,
            stk(new_p, 0), stk(new_p, 1), stk(new_p, 2), stk(new_p, 3), stk(new_p, 4),
            stk(new_s, 0), stk(new_s, 1), stk(new_s, 2), stk(new_s, 3), stk(new_s, 4))
```

```python
import functools

import numpy as np
import jax
import jax.numpy as jnp
from jax import lax
from jax.experimental import pallas as pl
from jax.experimental.pallas import tpu as pltpu

F32 = jnp.float32
BF16 = jnp.bfloat16
I32 = jnp.int32
I16 = jnp.int16

HEAD_DIM = 64
N_HEADS = 8
WIDTH = N_HEADS * HEAD_DIM
IDX_DIM = 64
CHUNK = 64
TOPK_MAX = 256
ROPE_THETA = 500000.0
EPS = 1e-6
NEG_INF = -1e30

LANES = 128
SUBLANES = 8
PACKED_ROWS = 16
HALF_BIAS = 1 << 15
PAIRS = WIDTH // LANES
VMEM_LIMIT_BYTES = 56 * 1024 * 1024

INT_MIN = -(2 ** 31)
INT_MAX = 2 ** 31 - 1
_NEG_BITS = int(np.array(NEG_INF, np.float32).view(np.int32))
NEG_KEY = _NEG_BITS ^ 0x7FFFFFFF

ROW_BLOCK = 256
KEY_BLOCK = 256
SB_TQ = 256
SCORE_ROWS = 128

_NT = (((1,), (1,)), ((), ()))


def _params(*sem):
    return pltpu.CompilerParams(dimension_semantics=sem, vmem_limit_bytes=VMEM_LIMIT_BYTES)


_G_QA, _G_KA, _G_VA, _G_UA, _G_QI = (i * WIDTH for i in range(5))
_G_KI = 5 * WIDTH
_G_WI = _G_KI + LANES
_G_QB = _G_WI + LANES
_G_KB, _G_VB, _G_UB = (_G_QB + i * WIDTH for i in range(1, 4))
_G_GA = _G_QB + 4 * WIDTH


def _rope(x, cs, s1, s2):
    outs = []
    for j in range(x.shape[1] // LANES):
        xj = x[:, j * LANES:(j + 1) * LANES]
        outs.append(xj * cs + pltpu.roll(xj, LANES - 8, 1) * s1 + pltpu.roll(xj, 8, 1) * s2)
    return outs[0] if len(outs) == 1 else jnp.concatenate(outs, axis=1)


def _proj_kernel(x_ref, g_ref, w_ref, cs_ref, s1_ref, s2_ref, qg_ref, kg_ref, ig_ref, bd_ref,
                 qa_o, ka_o, kab_o, va_o, vab_o, gta_o, qi_o, ki_o, kib_o, wi_o,
                 qb_o, kb_o, kbb_o, vb_o, vbb_o, gtb_o, sga_o, sgb_o, vat_o, *, d_model, rows_t):
    x = x_ref[...]
    ms = jnp.mean(x * x, axis=-1, keepdims=True)
    xn = (x * lax.rsqrt(ms + EPS) * g_ref[...]).astype(BF16)
    cs, s1, s2 = cs_ref[...], s1_ref[...], s2_ref[...]

    def proj(c0, width):
        return jnp.dot(xn, w_ref[:, c0:c0 + width], preferred_element_type=F32)

    def put_row(o_ref, val):
        if rows_t:
            o_ref[0] = val.T
        else:
            o_ref[...] = val

    def head_norm(h, gain):
        ss = jnp.dot((h * h).astype(BF16), bd_ref[...], preferred_element_type=F32)
        return h * lax.rsqrt(ss * (1.0 / HEAD_DIM) + EPS) * gain

    qa = _rope(head_norm(proj(_G_QA, WIDTH), qg_ref[...]), cs, s1, s2)
    qa_o[...] = (qa * (HEAD_DIM ** -0.5)).astype(BF16)
    ka = _rope(head_norm(proj(_G_KA, WIDTH), kg_ref[...]), cs, s1, s2)
    put_row(ka_o, ka)
    kab_o[...] = ka.astype(BF16)
    va = proj(_G_VA, WIDTH)
    put_row(va_o, va)
    vab_o[...] = va.astype(BF16)
    vat_o[0] = va.T.astype(BF16)
    ua = proj(_G_UA, WIDTH)
    gta_o[...] = (ua * jax.nn.sigmoid(ua)).astype(BF16)
    qi_o[...] = _rope(proj(_G_QI, WIDTH), cs, s1, s2).astype(BF16)

    kw = proj(_G_KI, 2 * LANES)
    hk = kw[:, :LANES]
    ssk = jnp.sum(hk * hk, axis=-1, keepdims=True)
    kin = hk * lax.rsqrt(ssk * (1.0 / IDX_DIM) + EPS) * ig_ref[...]
    ki_pad = _rope(kin, cs, s1, s2)
    ki = ki_pad[:, :IDX_DIM]
    if rows_t:
        ki_o[0] = ki_pad.T[:IDX_DIM]
    else:
        ki_o[...] = ki
    kib_o[...] = ki.astype(BF16)
    wi_o[...] = kw[:, LANES:] * (N_HEADS ** -0.5) * (IDX_DIM ** -0.5)

    qb_o[...] = (proj(_G_QB, WIDTH) * (HEAD_DIM ** -0.5)).astype(BF16)
    kb = proj(_G_KB, WIDTH)
    put_row(kb_o, kb)
    kbb_o[...] = kb.astype(BF16)
    vb = proj(_G_VB, WIDTH)
    put_row(vb_o, vb)
    vbb_o[...] = vb.astype(BF16)
    ub = proj(_G_UB, WIDTH)
    gtb_o[...] = (ub * jax.nn.sigmoid(ub)).astype(BF16)
    sga_o[...] = jax.nn.sigmoid(proj(_G_GA, d_model)).astype(BF16)
    sgb_o[...] = jax.nn.sigmoid(proj(_G_GA + d_model, d_model)).astype(BF16)


def _pack_w_in(w_in, d_model):
    o = 0
    parts = []

    def take(n, pad_to=None):
        nonlocal o
        blk = w_in[:, o:o + n]
        o += n
        if pad_to is not None and pad_to > n:
            blk = jnp.pad(blk, ((0, 0), (0, pad_to - n)))
        parts.append(blk)

    for _ in range(5):
        take(WIDTH)
    take(IDX_DIM, LANES)
    take(N_HEADS, LANES)
    for _ in range(4):
        take(WIDTH)
    take(d_model)
    take(d_model)
    assert o == w_in.shape[1]
    return jnp.concatenate(parts, axis=1).astype(BF16)


def _rope_tables(pos):
    half = HEAD_DIM // 8
    inv_freq = ROPE_THETA ** (-jnp.arange(half, dtype=F32) / half)
    ang = pos.astype(F32)[:, None] * inv_freq[None, :]
    cos, sin = jnp.cos(ang), jnp.sin(ang)
    t = pos.shape[0]
    ones = jnp.ones((t, HEAD_DIM - 2 * half), F32)
    zeros = jnp.zeros((t, HEAD_DIM - 2 * half), F32)
    z8 = jnp.zeros((t, half), F32)
    cs = jnp.concatenate([cos, cos, ones], axis=1)
    s1 = jnp.concatenate([-sin, z8, zeros], axis=1)
    s2 = jnp.concatenate([z8, sin, zeros], axis=1)
    rep = lambda a: jnp.concatenate([a, a], axis=1)
    return rep(cs), rep(s1), rep(s2)


def _project(x2d, pos, t_len, norm_g, w_pack, q_norm_g, k_norm_g, idx_k_norm_g, rows_t):
    n, d_model = x2d.shape
    rb = min(ROW_BLOCK, n)
    assert n % rb == 0
    cs, s1, s2 = _rope_tables(pos)
    if t_len >= rb:
        assert t_len % rb == 0
        tb = t_len // rb
        tab_map = lambda i: (i % tb, 0)
    else:
        assert rb % t_len == 0
        cs, s1, s2 = (jnp.tile(a, (rb // t_len, 1)) for a in (cs, s1, s2))
        tab_map = lambda i: (0, 0)
    qg = jnp.tile(q_norm_g, N_HEADS)[None, :]
    kg = jnp.tile(k_norm_g, N_HEADS)[None, :]
    ig = jnp.pad(idx_k_norm_g, (0, LANES - IDX_DIM))[None, :]
    hid = np.arange(WIDTH) // HEAD_DIM
    bd = jnp.asarray(hid[:, None] == hid[None, :], BF16)
    e_pack = w_pack.shape[1]

    row = lambda w: pl.BlockSpec((rb, w), lambda i: (i, 0))
    const = lambda shape: pl.BlockSpec(shape, lambda i: (0, 0))
    f32o = lambda w: jax.ShapeDtypeStruct((n, w), F32)
    if rows_t:
        assert t_len % rb == 0
        tb = t_len // rb
        rowo = lambda w: (jax.ShapeDtypeStruct((n // t_len, w, t_len), F32),
                          pl.BlockSpec((1, w, rb), lambda i: (i // tb, 0, i % tb)))
    else:
        rowo = lambda w: (f32o(w), row(w))
    b16o = lambda w: jax.ShapeDtypeStruct((n, w), BF16)
    outs = [
        (b16o(WIDTH), row(WIDTH)),
        rowo(WIDTH),
        (b16o(WIDTH), row(WIDTH)),
        rowo(WIDTH),
        (b16o(WIDTH), row(WIDTH)),
        (b16o(WIDTH), row(WIDTH)),
        (b16o(WIDTH), row(WIDTH)),
        rowo(IDX_DIM),
        (b16o(IDX_DIM), row(IDX_DIM)),
        (f32o(LANES), row(LANES)),
        (b16o(WIDTH), row(WIDTH)),
        rowo(WIDTH),
        (b16o(WIDTH), row(WIDTH)),
        rowo(WIDTH),
        (b16o(WIDTH), row(WIDTH)),
        (b16o(WIDTH), row(WIDTH)),
        (b16o(d_model), row(d_model)),
        (b16o(d_model), row(d_model)),
        (jax.ShapeDtypeStruct((n // rb, WIDTH, rb), BF16),
         pl.BlockSpec((1, WIDTH, rb), lambda i: (i, 0, 0))),
    ]
    return pl.pallas_call(
        functools.partial(_proj_kernel, d_model=d_model, rows_t=rows_t),
        grid=(n // rb,),
        in_specs=[row(d_model), const((1, d_model)), const((d_model, e_pack)),
                  pl.BlockSpec((rb, LANES), tab_map), pl.BlockSpec((rb, LANES), tab_map),
                  pl.BlockSpec((rb, LANES), tab_map),
                  const((1, WIDTH)), const((1, WIDTH)), const((1, LANES)), const((WIDTH, WIDTH))],
        out_specs=[o[1] for o in outs],
        out_shape=[o[0] for o in outs],
        compiler_params=_params("parallel"),
    )(x2d, norm_g[None, :], w_pack, cs, s1, s2, qg, kg, ig, bd)


def _sort_key(score):
    b = lax.bitcast_convert_type(score, I32)
    b = jnp.where(b == INT_MIN, 0, b)
    return jnp.where(b < 0, b ^ INT_MAX, b)


def _lane_tiles(x):
    return [x[:, c * LANES:(c + 1) * LANES] for c in range(x.shape[1] // LANES)]


def _chunk_limit(pos):
    shift = CHUNK.bit_length() - 1
    assert CHUNK == 1 << shift
    return ((pos >> shift) + 1) << shift


def _head_masks():
    lane = lax.broadcasted_iota(I32, (1, LANES), 1)
    return lane < HEAD_DIM


def _masked_heads(q):
    lo = _head_masks()
    out = []
    for qp in _lane_tiles(q):
        zero = jnp.zeros_like(qp)
        out += [jnp.where(lo, qp, zero), jnp.where(lo, zero, qp)]
    return out


def _transposed(q):
    return q.astype(F32).T.astype(BF16)


def _masked_heads_t(q):
    qt = _transposed(q)
    lo = lax.broadcasted_iota(I32, (LANES, 1), 0) < HEAD_DIM
    out = []
    for p in range(PAIRS):
        qp = qt[p * LANES:(p + 1) * LANES]
        zero = jnp.zeros_like(qp)
        out += [jnp.where(lo, qp, zero), jnp.where(lo, zero, qp)]
    return out


def _rows8(x):
    return x.reshape(x.shape[0] // SUBLANES, SUBLANES, x.shape[1])


def _fold_rows(x, op):
    return op(_rows8(x), axis=0)


def _index_queries(qi, wip):
    qt = _transposed(qi)
    q_heads = [qt[h * IDX_DIM:(h + 1) * IDX_DIM] for h in range(N_HEADS)]
    wt = wip.T
    tq = qi.shape[0]
    return q_heads, [jnp.broadcast_to(wt[h:h + 1, :], (SUBLANES, tq)) for h in range(N_HEADS)]


def _block_scores(ki_blk, q_heads, w_rows):
    parts = []
    for r0 in range(0, ki_blk.shape[0], SCORE_ROWS):
        sc = None
        for h in range(N_HEADS):
            d = jnp.dot(ki_blk[r0:r0 + SCORE_ROWS], q_heads[h], preferred_element_type=F32)
            t = jnp.maximum(_rows8(d), 0.0) * w_rows[h][None]
            sc = t if sc is None else sc + t
        parts.append(sc.reshape(SCORE_ROWS, sc.shape[2]))
    return jnp.concatenate(parts, axis=0)


def _key_to_float(key):
    return lax.bitcast_convert_type(jnp.where(key < 0, key ^ INT_MAX, key), F32)


def _select_bias(sc_scr, half_scr, bias_scr, nkb, limit, topk, n_unprocessed):
    _, kblk, tq = sc_scr.shape

    def count(pred):
        def body(kb, acc):
            return acc + _fold_rows(jnp.where(pred(sc_scr[kb], kb), 1, 0), jnp.sum)
        acc = lax.fori_loop(0, nkb, body, jnp.zeros((SUBLANES, tq), I32))
        return jnp.sum(acc, axis=0, keepdims=True)

    def count_ge(t):
        return count(lambda s, kb: s >= t) + jnp.where(t <= NEG_INF, n_unprocessed, 0)

    def count_gt(t):
        return count(lambda s, kb: s > t) + jnp.where(t < NEG_INF, n_unprocessed, 0)

    hi_scr, lo_scr = half_scr.at[0], half_scr.at[1]
    neg_hi, neg_lo = NEG_KEY >> 16, (NEG_KEY & 0xFFFF) - HALF_BIAS

    def split(kb, c):
        k = _sort_key(sc_scr[kb])
        hi_scr[kb] = (k >> 16).astype(I16)
        lo_scr[kb] = ((k & 0xFFFF) - HALF_BIAS).astype(I16)
        return c

    lax.fori_loop(0, nkb, split, 0)

    def count16(scr, cand):
        c16 = cand.astype(I16)

        def body(kb, acc):
            hit = jnp.where(scr[kb] >= c16, jnp.int16(1), jnp.int16(0))
            tiles = [hit[r:r + PACKED_ROWS] for r in range(0, kblk, PACKED_ROWS)]
            return acc + functools.reduce(jnp.add, tiles)
        acc = lax.fori_loop(0, nkb, body, jnp.zeros((PACKED_ROWS, tq), I16))
        return jnp.sum(acc.astype(I32), axis=0, keepdims=True)

    def search16(scr, target, extra):
        def step(i, t):
            cand = t + lax.shift_left(jnp.int32(1), 15 - i)
            return jnp.where(count16(scr, cand) + extra(cand) >= target, cand, t)
        return lax.fori_loop(0, 16, step, jnp.full((1, tq), -HALF_BIAS, I32))

    hi_extra = lambda cand: jnp.where(cand <= neg_hi, n_unprocessed, 0)
    thr_hi = search16(hi_scr, topk, hi_extra)
    above = count16(hi_scr, thr_hi + 1) + hi_extra(thr_hi + 1)
    thr_hi16 = thr_hi.astype(I16)

    def keep_matching(kb, c):
        lo_scr[kb] = jnp.where(hi_scr[kb] == thr_hi16, lo_scr[kb], jnp.int16(-HALF_BIAS))
        return c

    lax.fori_loop(0, nkb, keep_matching, 0)
    lo_extra = lambda cand: jnp.where((thr_hi == neg_hi) & (cand <= neg_lo), n_unprocessed, 0)
    thr_lo = search16(lo_scr, topk - above, lo_extra)
    thr = _key_to_float(lax.shift_left(thr_hi, 16) + (thr_lo + HALF_BIAS))
    c_gt = count_gt(thr)
    c_ge = count_ge(thr)

    def float_search():
        def step(i, key):
            cand = key + lax.shift_left(jnp.int32(1), 31 - i)
            return jnp.where(count_ge(_key_to_float(cand)) >= topk, cand, key)
        t = _key_to_float(lax.fori_loop(0, 32, step, jnp.full((1, tq), INT_MIN, I32)))
        return t, count_gt(t), count_ge(t)

    wrong = jnp.max(jnp.where((c_gt >= topk) | (c_ge < topk), 1, 0)) > 0
    thr, c_gt, c_ge = lax.cond(wrong, float_search, lambda: (thr, c_gt, c_ge))
    need = topk - c_gt
    excess = jnp.max(jnp.where(c_ge - c_gt > need, 1, 0)) > 0

    def key_index(kb):
        return kb * kblk + lax.broadcasted_iota(I32, (kblk, tq), 0)

    nbits = int(sc_scr.shape[0] * kblk).bit_length() + 1

    def tie_cut():
        def step(i, cut):
            cand = cut + lax.shift_left(jnp.int32(1), nbits - 1 - i)
            c = count(lambda s, kb: (s == thr) & (key_index(kb) < cand))
            return jnp.where(c <= need, cand, cut)
        return lax.fori_loop(0, nbits, step, jnp.zeros((1, tq), I32))

    cut = lax.cond(excess, tie_cut, lambda: jnp.full((1, tq), INT_MAX, I32))

    def write(kb, c):
        s = sc_scr[kb]
        idx = key_index(kb)
        sel = ((s > thr) | ((s == thr) & (idx < cut))) & (idx < limit)
        bias_scr[kb] = jnp.where(sel, 0.0, NEG_INF)
        return c

    lax.fori_loop(0, nkb, write, 0)


def _attention(qa, nkb, load, bias_scr, lg_scr, m_scr, mn_scr, l_scr, acc_scr):
    qz = _masked_heads_t(qa)
    m_scr[...] = jnp.full(m_scr.shape, NEG_INF, F32)
    l_scr[...] = jnp.zeros(l_scr.shape, F32)
    acc_scr[...] = jnp.zeros(acc_scr.shape, F32)

    def step(kb, k_pairs, vt):
        bias = bias_scr[kb]
        for h in range(N_HEADS):
            lg = jnp.dot(k_pairs[h // 2], qz[h], preferred_element_type=F32) + bias
            lg_scr[h] = lg
            blk_max = jnp.max(_fold_rows(lg, jnp.max), axis=0, keepdims=True)
            mn_scr[h] = jnp.maximum(m_scr[h], blk_max)
        for h in range(N_HEADS):
            m_new = mn_scr[h]
            alpha = jnp.exp(m_scr[h] - m_new)
            p = jnp.exp(_rows8(lg_scr[h]) - m_new[None])
            l_scr[h] = alpha * l_scr[h] + jnp.sum(jnp.sum(p, axis=0), axis=0, keepdims=True)
            pv = jnp.dot(vt[h * HEAD_DIM:(h + 1) * HEAD_DIM, :],
                         p.reshape(lg_scr.shape[1:]).astype(BF16),
                         preferred_element_type=F32)
            acc_scr[h] = (_rows8(acc_scr[h]) * alpha[None]).reshape(pv.shape) + pv
            m_scr[h] = m_new

    def body(kb, c):
        step(kb, *load(kb))
        return c

    lax.fori_loop(0, nkb, body, 0)
    outs = [(_rows8(acc_scr[h]) / l_scr[h][None]).reshape(acc_scr.shape[1:]) for h in range(N_HEADS)]
    return jnp.concatenate(outs, axis=0).T


def _attention_scratch(nkb, tq, kblk):
    return [pltpu.VMEM((nkb, kblk, tq), F32),
            pltpu.VMEM((2, nkb, kblk, tq), I16),
            pltpu.VMEM((nkb, kblk, tq), F32),
            pltpu.VMEM((N_HEADS, kblk, tq), F32),
            pltpu.VMEM((N_HEADS, SUBLANES, tq), F32),
            pltpu.VMEM((N_HEADS, SUBLANES, tq), F32),
            pltpu.VMEM((N_HEADS, SUBLANES, tq), F32),
            pltpu.VMEM((N_HEADS, HEAD_DIM, tq), F32)]


def _pair_blocks(ref, start, rows):
    return [ref[0, pl.ds(start, rows), p * LANES:(p + 1) * LANES] for p in range(PAIRS)]


def _dsa_prompt_kernel(qa_ref, qi_ref, wi_ref, gt_ref, ka_ref, vt_ref, ki_ref, o_ref,
                       sc_scr, half_scr, bias_scr, lg_scr, m_scr, mn_scr, l_scr, acc_scr, *, tq, topk, seq):
    kblk = tq
    j = pl.program_id(1)
    nkb = j + 1
    pos = j * tq + lax.broadcasted_iota(I32, (1, tq), 1)
    limit = _chunk_limit(pos)
    q_heads, w_rows = _index_queries(qi_ref[0], wi_ref[0])

    def score_body(kb, c):
        start = pl.multiple_of(kb * kblk, kblk)
        sc = _block_scores(ki_ref[0, pl.ds(start, kblk), :], q_heads, w_rows)
        kpos = kb * kblk + lax.broadcasted_iota(I32, (kblk, tq), 0)
        sc_scr[kb] = jnp.where(kpos < limit, sc, NEG_INF)
        return c

    lax.fori_loop(0, nkb, score_body, 0)
    _select_bias(sc_scr, half_scr, bias_scr, nkb, limit, topk, seq - nkb * kblk)

    def load(kb):
        return _pair_blocks(ka_ref, pl.multiple_of(kb * kblk, kblk), kblk), vt_ref[0, kb]

    oa = _attention(qa_ref[0], nkb, load, bias_scr, lg_scr, m_scr, mn_scr, l_scr, acc_scr)
    o_ref[0] = (oa * gt_ref[0].astype(F32)).astype(BF16)


def _dsa_prompt(qa, qi, wip, gta, kab, vat, kib, batch, seq):
    tq = ROW_BLOCK
    assert seq % tq == 0 and tq % CHUNK == 0
    topk = min(TOPK_MAX, seq // 4)
    nblk = seq // tq
    r3 = lambda a: a.reshape(batch, seq, a.shape[-1])
    qblk = lambda w: pl.BlockSpec((1, tq, w), lambda b, j: (b, j, 0))
    full = lambda w: pl.BlockSpec((1, seq, w), lambda b, j: (b, 0, 0))
    out = pl.pallas_call(
        functools.partial(_dsa_prompt_kernel, tq=tq, topk=topk, seq=seq),
        grid=(batch, nblk),
        in_specs=[qblk(WIDTH), qblk(WIDTH), qblk(LANES), qblk(WIDTH), full(WIDTH),
                  pl.BlockSpec((1, nblk, WIDTH, tq), lambda b, j: (b, 0, 0, 0)), full(IDX_DIM)],
        out_specs=qblk(WIDTH),
        out_shape=jax.ShapeDtypeStruct((batch, seq, WIDTH), BF16),
        scratch_shapes=_attention_scratch(nblk, tq, tq),
        compiler_params=_params("parallel", "arbitrary"),
    )(r3(qa), r3(qi), r3(wip), r3(gta), r3(kab), vat.reshape(batch, nblk, WIDTH, tq), r3(kib))
    return out.reshape(batch * seq, WIDTH)


def _pad_rows(x, rows):
    return jnp.concatenate([x, jnp.zeros((rows - x.shape[0], x.shape[1]), x.dtype)], axis=0)


def _dsa_sample_kernel(qa_ref, qi_ref, wi_ref, gt_ref, ka_ref, va_ref, ki_ref,
                       ckt_ref, cvt_ref, ckit_ref, o_ref, sc_scr, bias_scr, lg_scr,
                       *, tq, topk, past):
    total = past + tq
    width = past + LANES
    pos = past + lax.broadcasted_iota(I32, (tq, 1), 0)
    limit = jnp.minimum(_chunk_limit(pos), total)
    kpos = lax.broadcasted_iota(I32, (tq, width), 1)

    qi, wi = qi_ref[0], wi_ref[0]
    q_all = jnp.concatenate([qi[:, h * IDX_DIM:(h + 1) * IDX_DIM] for h in range(N_HEADS)], axis=0)
    w_all = jnp.concatenate([jnp.broadcast_to(wi[:, h:h + 1], (tq, LANES)) for h in range(N_HEADS)],
                            axis=0)

    def head_sum(d):
        t = jnp.maximum(d, 0.0) * w_all
        return functools.reduce(jnp.add, [t[h * tq:(h + 1) * tq] for h in range(N_HEADS)])

    kit = ckit_ref[0].astype(BF16)
    tiles = [head_sum(jnp.dot(q_all, kit[:, c:c + LANES], preferred_element_type=F32))
             for c in range(0, past, LANES)]
    tiles.append(head_sum(lax.dot_general(q_all, _pad_rows(ki_ref[0], LANES), _NT,
                                          preferred_element_type=F32)))
    scores = jnp.where(kpos < limit, jnp.concatenate(tiles, axis=1), NEG_INF)
    sc_scr[...] = jnp.where(kpos < total, scores, -jnp.inf)

    def count(pred):
        hit = jnp.where(pred(sc_scr[...]), 1, 0)
        return jnp.sum(functools.reduce(jnp.add, _lane_tiles(hit)), axis=1, keepdims=True)

    def search(i, key):
        cand = key + lax.shift_left(jnp.int32(1), 31 - i)
        cand_f = _key_to_float(cand)
        return jnp.where(count(lambda s: s >= cand_f) >= topk, cand, key)

    thr = _key_to_float(lax.fori_loop(0, 32, search, jnp.full((tq, 1), INT_MIN, I32)))
    c_gt = count(lambda s: s > thr)
    c_eq = count(lambda s: s == thr)
    need = topk - c_gt
    nbits = int(width).bit_length() + 1

    def tie_cut():
        def step(i, cut):
            cand = cut + lax.shift_left(jnp.int32(1), nbits - 1 - i)
            c = count(lambda s: (s == thr) & (kpos < cand))
            return jnp.where(c <= need, cand, cut)
        return lax.fori_loop(0, nbits, step, jnp.zeros((tq, 1), I32))

    excess = jnp.max(jnp.where(c_eq > need, 1, 0)) > 0
    cut = lax.cond(excess, tie_cut, lambda: jnp.full((tq, 1), INT_MAX, I32))
    s = sc_scr[...]
    sel = ((s > thr) | ((s == thr) & (kpos < cut))) & (kpos < limit)
    bias_scr[...] = jnp.where(sel, 0.0, NEG_INF)

    qa, ka_new, va_new = qa_ref[0], ka_ref[0], va_ref[0]
    maxima = []
    for h in range(N_HEADS):
        rows = slice(h * HEAD_DIM, (h + 1) * HEAD_DIM)
        q_h = qa[:, rows]
        k_new = _pad_rows(ka_new[:, rows], LANES)
        lg_c = jnp.dot(q_h, ckt_ref[0, rows, :].astype(BF16), preferred_element_type=F32)
        lg_c = lg_c + bias_scr[:, :past]
        lg_n = lax.dot_general(q_h, k_new, _NT, preferred_element_type=F32) + bias_scr[:, past:]
        lg_scr[h, :, :past] = lg_c
        lg_scr[h, :, past:] = lg_n
        maxima.append(jnp.maximum(jnp.max(lg_c, axis=1, keepdims=True),
                                  jnp.max(lg_n, axis=1, keepdims=True)))
    outs = []
    for h in range(N_HEADS):
        rows = slice(h * HEAD_DIM, (h + 1) * HEAD_DIM)
        v_new = _pad_rows(va_new[:, rows], LANES)
        p = jnp.exp(lg_scr[h] - maxima[h])
        l = jnp.sum(p, axis=1, keepdims=True)
        o = lax.dot_general(p[:, :past].astype(BF16), cvt_ref[0, rows, :].astype(BF16), _NT,
                            preferred_element_type=F32)
        o = o + jnp.dot(p[:, past:].astype(BF16), v_new, preferred_element_type=F32)
        outs.append(o / l)
    oa = jnp.concatenate(outs, axis=1)
    o_ref[0] = (oa * gt_ref[0].astype(F32)).astype(BF16)


def _dsa_sample(qa, qi, wip, gta, kab, vab, kib, cache_kt, cache_vt, cache_kit, batch, tq, past):
    assert past % LANES == 0 and tq <= LANES
    topk = min(TOPK_MAX, (past + tq) // 4)
    assert past + tq >= topk
    r3 = lambda a: a.reshape(batch, tq, a.shape[-1])
    new = lambda w: pl.BlockSpec((1, tq, w), lambda b: (b, 0, 0))
    cache = lambda rows: pl.BlockSpec((1, rows, past), lambda b: (b, 0, 0))
    out = pl.pallas_call(
        functools.partial(_dsa_sample_kernel, tq=tq, topk=topk, past=past),
        grid=(batch,),
        in_specs=[new(WIDTH), new(WIDTH), new(LANES), new(WIDTH),
                  new(WIDTH), new(WIDTH), new(IDX_DIM),
                  cache(WIDTH), cache(WIDTH), cache(IDX_DIM)],
        out_specs=new(WIDTH),
        out_shape=jax.ShapeDtypeStruct((batch, tq, WIDTH), BF16),
        scratch_shapes=[pltpu.VMEM((tq, past + LANES), F32), pltpu.VMEM((tq, past + LANES), F32),
                        pltpu.VMEM((N_HEADS, tq, past + LANES), F32)],
        compiler_params=_params("parallel"),
    )(r3(qa), r3(qi), r3(wip), r3(gta), r3(kab), r3(vab), r3(kib), cache_kt, cache_vt, cache_kit)
    return out.reshape(batch * tq, WIDTH)


SB_DEAD = -104.0


def _sb_step(z_of, pv_of, tri, stage_scr, carry_scr, acc_scr, mask):
    tq = carry_scr.shape[1]
    for h in range(N_HEADS):
        stage_scr[h] = z_of(h)
    live = None
    for h in range(N_HEADS):
        z = stage_scr[h]
        lg = jnp.log(1.0 + jnp.exp(-jnp.abs(z)))
        log1m = -(jnp.maximum(z, 0.0) + lg)
        if mask is not None:
            log1m = jnp.where(mask, log1m, 0.0)
        logsig = jnp.minimum(z, 0.0) - lg
        hi = log1m.astype(BF16)
        lo = (log1m - hi.astype(F32)).astype(BF16)
        after = jnp.dot(jnp.concatenate([hi, lo], axis=1), tri, preferred_element_type=F32)
        carry = carry_scr[h]
        stage_scr[h] = jnp.concatenate([t + carry for t in _lane_tiles(logsig + after)], axis=1)
        carry = carry + jnp.broadcast_to(jnp.sum(log1m, axis=1, keepdims=True), (tq, LANES))
        carry_scr[h] = carry
        live = carry if live is None else jnp.maximum(live, carry)
    for h in range(N_HEADS):
        a = jnp.exp(stage_scr[h])
        if mask is not None:
            a = jnp.where(mask, a, 0.0)
        acc_scr[h] += pv_of(h, a.astype(BF16))
    return (jnp.max(live) > SB_DEAD).astype(I32)


def _sb_pair_fns(qz, k_pairs, v_pairs):
    z_of = lambda h: lax.dot_general(qz[h], k_pairs[h // 2], _NT, preferred_element_type=F32)
    pv_of = lambda h, a: jnp.dot(a, v_pairs[h // 2], preferred_element_type=F32)
    return z_of, pv_of


def _sb_finish(acc_scr, gt):
    lo = _head_masks()
    outs = [jnp.where(lo, acc_scr[2 * p], acc_scr[2 * p + 1]) for p in range(PAIRS)]
    return (jnp.concatenate(outs, axis=1) * gt.astype(F32)).astype(BF16)


def _sb_tri(kblk):
    jj = np.arange(kblk)
    m = (jj[:, None] > jj[None, :])
    return jnp.asarray(np.concatenate([m, m], axis=0), BF16)


def _sb_scratch(tq, kblk):
    return [pltpu.VMEM((N_HEADS, tq, kblk), F32),
            pltpu.VMEM((N_HEADS, tq, LANES), F32),
            pltpu.VMEM((N_HEADS, tq, LANES), F32)]


def _sb_prompt_kernel(qb_ref, gt_ref, kb_ref, vb_ref, tri_ref, o_ref,
                      stage_scr, carry_scr, acc_scr, *, tq):
    j = pl.program_id(1)
    qz = _masked_heads(qb_ref[0])
    tri = tri_ref[...]
    diag = (lax.broadcasted_iota(I32, (tq, tq), 1) < lax.broadcasted_iota(I32, (tq, tq), 0))
    carry_scr[...] = jnp.zeros(carry_scr.shape, F32)
    acc_scr[...] = jnp.zeros(acc_scr.shape, F32)

    def step(kb, mask):
        start = pl.multiple_of(kb * tq, tq)
        fns = _sb_pair_fns(qz, _pair_blocks(kb_ref, start, tq), _pair_blocks(vb_ref, start, tq))
        return _sb_step(*fns, tri, stage_scr, carry_scr, acc_scr, mask)

    live = step(j, diag)
    lax.while_loop(lambda st: (st[0] >= 0) & (st[1] > 0),
                   lambda st: (st[0] - 1, step(st[0], None)), (j - 1, live))
    o_ref[0] = _sb_finish(acc_scr, gt_ref[0])


def _sb_prompt(qb, gtb, kbb, vbb, batch, seq):
    tq = min(SB_TQ, seq)
    assert seq % tq == 0
    r3 = lambda a: a.reshape(batch, seq, a.shape[-1])
    qblk = pl.BlockSpec((1, tq, WIDTH), lambda b, j: (b, j, 0))
    full = pl.BlockSpec((1, seq, WIDTH), lambda b, j: (b, 0, 0))
    out = pl.pallas_call(
        functools.partial(_sb_prompt_kernel, tq=tq),
        grid=(batch, seq // tq),
        in_specs=[qblk, qblk, full, full, pl.BlockSpec((2 * tq, tq), lambda b, j: (0, 0))],
        out_specs=qblk,
        out_shape=jax.ShapeDtypeStruct((batch, seq, WIDTH), BF16),
        scratch_shapes=_sb_scratch(tq, tq),
        compiler_params=_params("parallel", "arbitrary"),
    )(r3(qb), r3(gtb), r3(kbb), r3(vbb), _sb_tri(tq))
    return out.reshape(batch * seq, WIDTH)


def _sb_sample_kernel(qb_ref, gt_ref, kb_ref, vb_ref, ckt_ref, cvt_ref, tri_ref, o_ref,
                      stage_scr, carry_scr, acc_scr, *, tq, kblk, past):
    ncache = past // kblk
    qb = qb_ref[0]
    tri = tri_ref[...]
    diag = (lax.broadcasted_iota(I32, (tq, kblk), 1) < lax.broadcasted_iota(I32, (tq, kblk), 0))
    carry_scr[...] = jnp.zeros(carry_scr.shape, F32)
    acc_scr[...] = jnp.zeros(acc_scr.shape, F32)
    fns = _sb_pair_fns(_masked_heads(qb), [_pad_rows(t, kblk) for t in _lane_tiles(kb_ref[0])],
                       [_pad_rows(t, kblk) for t in _lane_tiles(vb_ref[0])])
    live = _sb_step(*fns, tri, stage_scr, carry_scr, acc_scr, diag)
    q_heads = [qb[:, h * HEAD_DIM:(h + 1) * HEAD_DIM] for h in range(N_HEADS)]
    zeros = jnp.zeros((tq, HEAD_DIM), F32)

    def cache_step(kb):
        cols = slice(kb * kblk, (kb + 1) * kblk)

        def z_of(h):
            kt = ckt_ref[0, h * HEAD_DIM:(h + 1) * HEAD_DIM, cols].astype(BF16)
            return jnp.dot(q_heads[h], kt, preferred_element_type=F32)

        def pv_of(h, a):
            vt = cvt_ref[0, h * HEAD_DIM:(h + 1) * HEAD_DIM, cols].astype(BF16)
            pv = lax.dot_general(a, vt, _NT, preferred_element_type=F32)
            return jnp.concatenate([pv, zeros] if h % 2 == 0 else [zeros, pv], axis=1)

        return _sb_step(z_of, pv_of, tri, stage_scr, carry_scr, acc_scr, None)

    for kb in reversed(range(ncache)):
        live = lax.cond(live > 0, functools.partial(cache_step, kb), lambda: jnp.int32(0))
    o_ref[0] = _sb_finish(acc_scr, gt_ref[0])


def _sb_sample(qb, gtb, kbb, vbb, cache_kt, cache_vt, batch, tq, past):
    kblk = min(KEY_BLOCK, past)
    assert past % kblk == 0 and tq <= kblk
    r3 = lambda a: a.reshape(batch, tq, a.shape[-1])
    new = pl.BlockSpec((1, tq, WIDTH), lambda b: (b, 0, 0))
    cache = pl.BlockSpec((1, WIDTH, past), lambda b: (b, 0, 0))
    out = pl.pallas_call(
        functools.partial(_sb_sample_kernel, tq=tq, kblk=kblk, past=past),
        grid=(batch,),
        in_specs=[new, new, new, new, cache, cache, pl.BlockSpec((2 * kblk, kblk), lambda b: (0, 0))],
        out_specs=new,
        out_shape=jax.ShapeDtypeStruct((batch, tq, WIDTH), BF16),
        scratch_shapes=_sb_scratch(tq, kblk),
        compiler_params=_params("parallel"),
    )(r3(qb), r3(gtb), r3(kbb), r3(vbb), cache_kt, cache_vt, _sb_tri(kblk))
    return out.reshape(batch * tq, WIDTH)


def _merge_kernel(x_ref, ta_ref, tb_ref, sga_ref, sgb_ref, wa_ref, wb_ref, wo_ref, y_ref):
    ya = jnp.dot(ta_ref[...], wa_ref[...], preferred_element_type=F32)
    yb = jnp.dot(tb_ref[...], wb_ref[...], preferred_element_type=F32)
    mixed = sga_ref[...].astype(F32) * ya + sgb_ref[...].astype(F32) * yb
    y_ref[...] = x_ref[...] + jnp.dot(mixed.astype(BF16), wo_ref[...], preferred_element_type=F32)


def _merge(x2d, ta, tb, sga, sgb, wa, wb, wo):
    n, d_model = x2d.shape
    rb = min(ROW_BLOCK, n)
    row = lambda w: pl.BlockSpec((rb, w), lambda i: (i, 0))
    const = lambda a: pl.BlockSpec(a.shape, lambda i: (0, 0))
    return pl.pallas_call(
        _merge_kernel,
        grid=(n // rb,),
        in_specs=[row(d_model), row(WIDTH), row(WIDTH), row(d_model), row(d_model),
                  const(wa), const(wb), const(wo)],
        out_specs=row(d_model),
        out_shape=jax.ShapeDtypeStruct((n, d_model), F32),
        compiler_params=_params("parallel"),
    )(x2d, ta, tb, sga, sgb, wa, wb, wo)


def _layer(x, pos, past, params):
    norm_g, w_pack, q_norm_g, k_norm_g, idx_k_norm_g, wa, wb, wo = params
    b, t, d_model = x.shape
    x2d = x.reshape(b * t, d_model)
    (qa, ka, kab, va, vab, gta, qi, ki, kib, wi, qb, kb, kbb, vb, vbb, gtb, sga, sgb, vat) = _project(
        x2d, pos, t, norm_g, w_pack, q_norm_g, k_norm_g, idx_k_norm_g, rows_t=past is None)
    if past is None:
        ta = _dsa_prompt(qa, qi, wi, gta, kab, vat, kib, b, t)
        tb = _sb_prompt(qb, gtb, kbb, vbb, b, t)
    else:
        p_len = past[0].shape[1]
        keys_minor = lambda c: jnp.moveaxis(c, 1, -1).reshape(b, -1, p_len)
        c_ak, c_av, c_ik, c_bk, c_bv = (keys_minor(c) for c in past)
        ta = _dsa_sample(qa, qi, wi, gta, kab, vab, kib, c_ak, c_av, c_ik, b, t, p_len)
        tb = _sb_sample(qb, gtb, kbb, vbb, c_bk, c_bv, b, t, p_len)
    y = _merge(x2d, ta, tb, sga, sgb, wa, wb, wo).reshape(b, t, d_model)
    if past is None:
        heads = lambda a: a.reshape(b, N_HEADS, HEAD_DIM, t).transpose(0, 3, 1, 2)
        rows = (heads(ka), heads(va), ki.transpose(0, 2, 1), heads(kb), heads(vb))
    else:
        heads = lambda a: a.reshape(b, t, N_HEADS, HEAD_DIM)
        rows = (heads(ka), heads(va), ki.reshape(b, t, IDX_DIM), heads(kb), heads(vb))
    return y, rows


def kernel(x_prompt, x_sample, cache_a_k, cache_a_v, cache_idx_k, cache_b_k, cache_b_v,
           norm_g, w_in, q_norm_g, k_norm_g, idx_k_norm_g, w_a_out, w_b_out, w_o):
    depth = norm_g.shape[0]
    d_model = x_prompt.shape[2]
    seq = x_prompt.shape[1]
    past_len = cache_a_k.shape[2]
    dec_seq = x_sample.shape[1]
    pos_p = jnp.arange(seq, dtype=I32)
    pos_s = past_len + jnp.arange(dec_seq, dtype=I32)
    yp, ys = x_prompt, x_sample
    new_p, new_s = [], []
    for l in range(depth):
        params = (norm_g[l], _pack_w_in(w_in[l], d_model), q_norm_g[l], k_norm_g[l], idx_k_norm_g[l],
                  w_a_out[l].astype(BF16), w_b_out[l].astype(BF16), w_o[l].astype(BF16))
        yp, rows_p = _layer(yp, pos_p, None, params)
        past = (cache_a_k[l], cache_a_v[l], cache_idx_k[l], cache_b_k[l], cache_b_v[l])
        ys, rows_s = _layer(ys, pos_s, past, params)
        new_p.append(rows_p)
        new_s.append(rows_s)
    stk = lambda rows, i: jnp.stack([r[i] for r in rows], axis=0)
    return (yp, ys,
            stk(new_p, 0), stk(new_p, 1), stk(new_p, 2), stk(new_p, 3), stk(new_p, 4),
            stk(new_s, 0), stk(new_s, 1), stk(new_s, 2), stk(new_s, 3), stk(new_s, 4))
```

```python
import functools

import numpy as np
import jax
import jax.numpy as jnp
from jax import lax
from jax.experimental import pallas as pl
from jax.experimental.pallas import tpu as pltpu

F32 = jnp.float32
BF16 = jnp.bfloat16
I32 = jnp.int32
I16 = jnp.int16

HEAD_DIM = 64
N_HEADS = 8
WIDTH = N_HEADS * HEAD_DIM
IDX_DIM = 64
CHUNK = 64
TOPK_MAX = 256
ROPE_THETA = 500000.0
EPS = 1e-6
NEG_INF = -1e30

LANES = 128
SUBLANES = 8
PACKED_ROWS = 16
HALF_BIAS = 1 << 15
PAIRS = WIDTH // LANES
VMEM_LIMIT_BYTES = 56 * 1024 * 1024

INT_MIN = -(2 ** 31)
INT_MAX = 2 ** 31 - 1
_NEG_BITS = int(np.array(NEG_INF, np.float32).view(np.int32))
NEG_KEY = _NEG_BITS ^ 0x7FFFFFFF

PROJ_ROWS = 512
MERGE_ROWS = 512
KEY_BLOCK = 256
SB_TQ = 256
SCORE_ROWS = 128
ATTN_ROWS = 128

_NT = (((1,), (1,)), ((), ()))


def _params(*sem):
    return pltpu.CompilerParams(dimension_semantics=sem, vmem_limit_bytes=VMEM_LIMIT_BYTES)


_G_QA, _G_KA, _G_VA, _G_UA, _G_QI = (i * WIDTH for i in range(5))
_G_KI = 5 * WIDTH
_G_WI = _G_KI + LANES
_G_QB = _G_WI + LANES
_G_KB, _G_VB, _G_UB = (_G_QB + i * WIDTH for i in range(1, 4))
_G_GA = _G_QB + 4 * WIDTH


def _rope(x, cs, s1, s2):
    outs = []
    for j in range(x.shape[1] // LANES):
        xj = x[:, j * LANES:(j + 1) * LANES]
        outs.append(xj * cs + pltpu.roll(xj, LANES - 8, 1) * s1 + pltpu.roll(xj, 8, 1) * s2)
    return outs[0] if len(outs) == 1 else jnp.concatenate(outs, axis=1)


def _proj_kernel(x_ref, g_ref, w_ref, cs_ref, s1_ref, s2_ref, qg_ref, kg_ref, ig_ref, bd_ref,
                 qa_o, ka_o, kab_o, va_o, vab_o, gta_o, qi_o, ki_o, kib_o, wi_o,
                 qb_o, kb_o, kbb_o, vb_o, vbb_o, gtb_o, sga_o, sgb_o, vat_o, *, d_model, rows_t):
    x = x_ref[...]
    ms = jnp.mean(x * x, axis=-1, keepdims=True)
    xn = (x * lax.rsqrt(ms + EPS) * g_ref[...]).astype(BF16)
    cs, s1, s2 = cs_ref[...], s1_ref[...], s2_ref[...]

    def proj(c0, width):
        return jnp.dot(xn, w_ref[:, c0:c0 + width], preferred_element_type=F32)

    def put_row(o_ref, val):
        if rows_t:
            o_ref[0] = val.T
        else:
            o_ref[...] = val

    def head_norm(h, gain):
        ss = jnp.dot((h * h).astype(BF16), bd_ref[...], preferred_element_type=F32)
        return h * lax.rsqrt(ss * (1.0 / HEAD_DIM) + EPS) * gain

    qa = _rope(head_norm(proj(_G_QA, WIDTH), qg_ref[...]), cs, s1, s2)
    qa_o[...] = (qa * (HEAD_DIM ** -0.5)).astype(BF16)
    ka = _rope(head_norm(proj(_G_KA, WIDTH), kg_ref[...]), cs, s1, s2)
    put_row(ka_o, ka)
    kab_o[...] = ka.astype(BF16)
    va = proj(_G_VA, WIDTH)
    put_row(va_o, va)
    vab_o[...] = va.astype(BF16)
    vat = va.T.astype(BF16)
    vt_cols = vat_o.shape[2]
    for c in range(vat_o.shape[0]):
        vat_o[c] = vat[:, c * vt_cols:(c + 1) * vt_cols]
    ua = proj(_G_UA, WIDTH)
    gta_o[...] = (ua * jax.nn.sigmoid(ua)).astype(BF16)
    qi_o[...] = _rope(proj(_G_QI, WIDTH), cs, s1, s2).astype(BF16)

    kw = proj(_G_KI, 2 * LANES)
    hk = kw[:, :LANES]
    ssk = jnp.sum(hk * hk, axis=-1, keepdims=True)
    kin = hk * lax.rsqrt(ssk * (1.0 / IDX_DIM) + EPS) * ig_ref[...]
    ki_pad = _rope(kin, cs, s1, s2)
    ki = ki_pad[:, :IDX_DIM]
    if rows_t:
        ki_o[0] = ki_pad.T[:IDX_DIM]
    else:
        ki_o[...] = ki
    kib_o[...] = ki.astype(BF16)
    wi_o[...] = kw[:, LANES:] * (N_HEADS ** -0.5) * (IDX_DIM ** -0.5)

    qb_o[...] = (proj(_G_QB, WIDTH) * (HEAD_DIM ** -0.5)).astype(BF16)
    kb = proj(_G_KB, WIDTH)
    put_row(kb_o, kb)
    kbb_o[...] = kb.astype(BF16)
    vb = proj(_G_VB, WIDTH)
    put_row(vb_o, vb)
    vbb_o[...] = vb.astype(BF16)
    ub = proj(_G_UB, WIDTH)
    gtb_o[...] = (ub * jax.nn.sigmoid(ub)).astype(BF16)
    sga_o[...] = jax.nn.sigmoid(proj(_G_GA, d_model)).astype(BF16)
    sgb_o[...] = jax.nn.sigmoid(proj(_G_GA + d_model, d_model)).astype(BF16)


def _pack_w_in(w_in, d_model):
    o = 0
    parts = []

    def take(n, pad_to=None):
        nonlocal o
        blk = w_in[:, o:o + n]
        o += n
        if pad_to is not None and pad_to > n:
            blk = jnp.pad(blk, ((0, 0), (0, pad_to - n)))
        parts.append(blk)

    for _ in range(5):
        take(WIDTH)
    take(IDX_DIM, LANES)
    take(N_HEADS, LANES)
    for _ in range(4):
        take(WIDTH)
    take(d_model)
    take(d_model)
    assert o == w_in.shape[1]
    return jnp.concatenate(parts, axis=1).astype(BF16)


def _rope_tables(pos):
    half = HEAD_DIM // 8
    inv_freq = ROPE_THETA ** (-jnp.arange(half, dtype=F32) / half)
    ang = pos.astype(F32)[:, None] * inv_freq[None, :]
    cos, sin = jnp.cos(ang), jnp.sin(ang)
    t = pos.shape[0]
    ones = jnp.ones((t, HEAD_DIM - 2 * half), F32)
    zeros = jnp.zeros((t, HEAD_DIM - 2 * half), F32)
    z8 = jnp.zeros((t, half), F32)
    cs = jnp.concatenate([cos, cos, ones], axis=1)
    s1 = jnp.concatenate([-sin, z8, zeros], axis=1)
    s2 = jnp.concatenate([z8, sin, zeros], axis=1)
    rep = lambda a: jnp.concatenate([a, a], axis=1)
    return rep(cs), rep(s1), rep(s2)


def _project(x2d, pos, t_len, norm_g, w_pack, q_norm_g, k_norm_g, idx_k_norm_g, rows_t):
    n, d_model = x2d.shape
    rb = min(PROJ_ROWS, n)
    vt_cols = min(KEY_BLOCK, rb)
    assert n % rb == 0 and rb % vt_cols == 0
    cs, s1, s2 = _rope_tables(pos)
    if t_len >= rb:
        assert t_len % rb == 0
        tb = t_len // rb
        tab_map = lambda i: (i % tb, 0)
    else:
        assert rb % t_len == 0
        cs, s1, s2 = (jnp.tile(a, (rb // t_len, 1)) for a in (cs, s1, s2))
        tab_map = lambda i: (0, 0)
    qg = jnp.tile(q_norm_g, N_HEADS)[None, :]
    kg = jnp.tile(k_norm_g, N_HEADS)[None, :]
    ig = jnp.pad(idx_k_norm_g, (0, LANES - IDX_DIM))[None, :]
    hid = np.arange(WIDTH) // HEAD_DIM
    bd = jnp.asarray(hid[:, None] == hid[None, :], BF16)
    e_pack = w_pack.shape[1]

    row = lambda w: pl.BlockSpec((rb, w), lambda i: (i, 0))
    const = lambda shape: pl.BlockSpec(shape, lambda i: (0, 0))
    f32o = lambda w: jax.ShapeDtypeStruct((n, w), F32)
    if rows_t:
        assert t_len % rb == 0
        tb = t_len // rb
        rowo = lambda w: (jax.ShapeDtypeStruct((n // t_len, w, t_len), F32),
                          pl.BlockSpec((1, w, rb), lambda i: (i // tb, 0, i % tb)))
    else:
        rowo = lambda w: (f32o(w), row(w))
    b16o = lambda w: jax.ShapeDtypeStruct((n, w), BF16)
    outs = [
        (b16o(WIDTH), row(WIDTH)),
        rowo(WIDTH),
        (b16o(WIDTH), row(WIDTH)),
        rowo(WIDTH),
        (b16o(WIDTH), row(WIDTH)),
        (b16o(WIDTH), row(WIDTH)),
        (b16o(WIDTH), row(WIDTH)),
        rowo(IDX_DIM),
        (b16o(IDX_DIM), row(IDX_DIM)),
        (f32o(LANES), row(LANES)),
        (b16o(WIDTH), row(WIDTH)),
        rowo(WIDTH),
        (b16o(WIDTH), row(WIDTH)),
        rowo(WIDTH),
        (b16o(WIDTH), row(WIDTH)),
        (b16o(WIDTH), row(WIDTH)),
        (b16o(d_model), row(d_model)),
        (b16o(d_model), row(d_model)),
        (jax.ShapeDtypeStruct((n // vt_cols, WIDTH, vt_cols), BF16),
         pl.BlockSpec((rb // vt_cols, WIDTH, vt_cols), lambda i: (i, 0, 0))),
    ]
    return pl.pallas_call(
        functools.partial(_proj_kernel, d_model=d_model, rows_t=rows_t),
        grid=(n // rb,),
        in_specs=[row(d_model), const((1, d_model)),
                  pl.BlockSpec((d_model, e_pack), lambda i: (0, 0), pipeline_mode=pl.Buffered(1)),
                  pl.BlockSpec((rb, LANES), tab_map), pl.BlockSpec((rb, LANES), tab_map),
                  pl.BlockSpec((rb, LANES), tab_map),
                  const((1, WIDTH)), const((1, WIDTH)), const((1, LANES)), const((WIDTH, WIDTH))],
        out_specs=[o[1] for o in outs],
        out_shape=[o[0] for o in outs],
        compiler_params=_params("parallel"),
    )(x2d, norm_g[None, :], w_pack, cs, s1, s2, qg, kg, ig, bd)


def _sort_key(score):
    b = lax.bitcast_convert_type(score, I32)
    b = jnp.where(b == INT_MIN, 0, b)
    return jnp.where(b < 0, b ^ INT_MAX, b)


def _lane_tiles(x):
    return [x[:, c * LANES:(c + 1) * LANES] for c in range(x.shape[1] // LANES)]


def _chunk_limit(pos):
    shift = CHUNK.bit_length() - 1
    assert CHUNK == 1 << shift
    return ((pos >> shift) + 1) << shift


def _head_masks():
    lane = lax.broadcasted_iota(I32, (1, LANES), 1)
    return lane < HEAD_DIM


def _masked_heads(q):
    lo = _head_masks()
    out = []
    for qp in _lane_tiles(q):
        zero = jnp.zeros_like(qp)
        out += [jnp.where(lo, qp, zero), jnp.where(lo, zero, qp)]
    return out


def _transposed(q):
    return q.astype(F32).T.astype(BF16)


def _masked_heads_t(q):
    qt = _transposed(q)
    lo = lax.broadcasted_iota(I32, (LANES, 1), 0) < HEAD_DIM
    out = []
    for p in range(PAIRS):
        qp = qt[p * LANES:(p + 1) * LANES]
        zero = jnp.zeros_like(qp)
        out += [jnp.where(lo, qp, zero), jnp.where(lo, zero, qp)]
    return out


def _rows8(x):
    return x.reshape(x.shape[0] // SUBLANES, SUBLANES, x.shape[1])


def _fold_rows(x, op):
    return op(_rows8(x), axis=0)


def _index_queries(qi, wip):
    qt = _transposed(qi)
    q_heads = [qt[h * IDX_DIM:(h + 1) * IDX_DIM] for h in range(N_HEADS)]
    wt = wip.T
    tq = qi.shape[0]
    return q_heads, [jnp.broadcast_to(wt[h:h + 1, :], (SUBLANES, tq)) for h in range(N_HEADS)]


def _block_scores(ki_blk, q_heads, w_rows):
    parts = []
    for r0 in range(0, ki_blk.shape[0], SCORE_ROWS):
        sc = None
        for h in range(N_HEADS):
            d = jnp.dot(ki_blk[r0:r0 + SCORE_ROWS], q_heads[h], preferred_element_type=F32)
            t = jnp.maximum(_rows8(d), 0.0) * w_rows[h][None]
            sc = t if sc is None else sc + t
        parts.append(sc.reshape(SCORE_ROWS, sc.shape[2]))
    return jnp.concatenate(parts, axis=0)


def _key_to_float(key):
    return lax.bitcast_convert_type(jnp.where(key < 0, key ^ INT_MAX, key), F32)


def _select_bias(sc_scr, half_scr, bias_scr, nkb, limit, topk, n_unprocessed):
    _, kblk, tq = sc_scr.shape

    def count(pred):
        def body(kb, acc):
            return acc + _fold_rows(jnp.where(pred(sc_scr[kb], kb), 1, 0), jnp.sum)
        acc = lax.fori_loop(0, nkb, body, jnp.zeros((SUBLANES, tq), I32))
        return jnp.sum(acc, axis=0, keepdims=True)

    def count_ge(t):
        return count(lambda s, kb: s >= t) + jnp.where(t <= NEG_INF, n_unprocessed, 0)

    def count_gt(t):
        return count(lambda s, kb: s > t) + jnp.where(t < NEG_INF, n_unprocessed, 0)

    hi_scr, lo_scr = half_scr.at[0], half_scr.at[1]
    neg_hi, neg_lo = NEG_KEY >> 16, (NEG_KEY & 0xFFFF) - HALF_BIAS

    def split(kb, c):
        k = _sort_key(sc_scr[kb])
        hi_scr[kb] = (k >> 16).astype(I16)
        lo_scr[kb] = ((k & 0xFFFF) - HALF_BIAS).astype(I16)
        return c

    lax.fori_loop(0, nkb, split, 0)
    lowest = jnp.full((kblk, tq), -HALF_BIAS, I16)
    hi_scr[nkb] = lowest
    lo_scr[nkb] = lowest
    npairs = (nkb + 1) // 2

    def count16(scr, cand):
        c16 = cand.astype(I16)

        def body(kp, acc):
            for u in range(2):
                hit = jnp.where(scr[2 * kp + u] >= c16, jnp.int16(1), jnp.int16(0))
                tiles = [hit[r:r + PACKED_ROWS] for r in range(0, kblk, PACKED_ROWS)]
                acc = acc + functools.reduce(jnp.add, tiles)
            return acc
        acc = lax.fori_loop(0, npairs, body, jnp.zeros((PACKED_ROWS, tq), I16))
        return jnp.sum(acc.astype(I32), axis=0, keepdims=True)

    def search16(scr, target, extra):
        def step(i, t):
            cand = t + lax.shift_left(jnp.int32(1), 15 - i)
            return jnp.where(count16(scr, cand) + extra(cand) >= target, cand, t)
        return lax.fori_loop(0, 16, step, jnp.full((1, tq), -HALF_BIAS, I32))

    hi_extra = lambda cand: jnp.where(cand <= neg_hi, n_unprocessed, 0)
    thr_hi = search16(hi_scr, topk, hi_extra)
    above = count16(hi_scr, thr_hi + 1) + hi_extra(thr_hi + 1)
    thr_hi16 = thr_hi.astype(I16)

    def keep_matching(kb, c):
        lo_scr[kb] = jnp.where(hi_scr[kb] == thr_hi16, lo_scr[kb], jnp.int16(-HALF_BIAS))
        return c

    lax.fori_loop(0, nkb, keep_matching, 0)
    lo_extra = lambda cand: jnp.where((thr_hi == neg_hi) & (cand <= neg_lo), n_unprocessed, 0)
    thr_lo = search16(lo_scr, topk - above, lo_extra)
    thr = _key_to_float(lax.shift_left(thr_hi, 16) + (thr_lo + HALF_BIAS))
    c_gt = count_gt(thr)
    c_ge = count_ge(thr)

    def float_search():
        def step(i, key):
            cand = key + lax.shift_left(jnp.int32(1), 31 - i)
            return jnp.where(count_ge(_key_to_float(cand)) >= topk, cand, key)
        t = _key_to_float(lax.fori_loop(0, 32, step, jnp.full((1, tq), INT_MIN, I32)))
        return t, count_gt(t), count_ge(t)

    wrong = jnp.max(jnp.where((c_gt >= topk) | (c_ge < topk), 1, 0)) > 0
    thr, c_gt, c_ge = lax.cond(wrong, float_search, lambda: (thr, c_gt, c_ge))
    need = topk - c_gt
    excess = jnp.max(jnp.where(c_ge - c_gt > need, 1, 0)) > 0

    def key_index(kb):
        return kb * kblk + lax.broadcasted_iota(I32, (kblk, tq), 0)

    nbits = int(sc_scr.shape[0] * kblk).bit_length() + 1

    def tie_cut():
        def step(i, cut):
            cand = cut + lax.shift_left(jnp.int32(1), nbits - 1 - i)
            c = count(lambda s, kb: (s == thr) & (key_index(kb) < cand))
            return jnp.where(c <= need, cand, cut)
        return lax.fori_loop(0, nbits, step, jnp.zeros((1, tq), I32))

    cut = lax.cond(excess, tie_cut, lambda: jnp.full((1, tq), INT_MAX, I32))

    def write(kb, c):
        s = sc_scr[kb]
        idx = key_index(kb)
        sel = ((s > thr) | ((s == thr) & (idx < cut))) & (idx < limit)
        bias_scr[kb] = jnp.where(sel, 0.0, NEG_INF)
        return c

    lax.fori_loop(0, nkb, write, 0)


def _attention(qa, nkb, load, bias_scr, lg_scr, m_scr, mn_scr, l_scr, acc_scr):
    qz = _masked_heads_t(qa)
    kblk = lg_scr.shape[1]
    ones = jnp.ones((PACKED_ROWS, kblk), BF16)
    m_scr[...] = jnp.full(m_scr.shape, NEG_INF, F32)
    l_scr[...] = jnp.zeros(l_scr.shape, F32)
    acc_scr[...] = jnp.zeros(acc_scr.shape, F32)

    def step(kb, k_pairs, vt):
        bias = bias_scr[kb]
        for h in range(N_HEADS):
            lg = jnp.dot(k_pairs[h // 2], qz[h], preferred_element_type=F32) + bias
            lg_scr[h] = lg
            blk_max = jnp.max(_fold_rows(lg, jnp.max), axis=0, keepdims=True)
            mn_scr[h] = jnp.maximum(m_scr[h], blk_max)
        same = jnp.minimum(kb, 0)
        for h in range(N_HEADS):
            m_new = mn_scr[h]
            alpha = jnp.exp(m_scr[h] - m_new)
            vt_h = jnp.concatenate([vt[h * HEAD_DIM:(h + 1) * HEAD_DIM, :], ones], axis=0)
            pv = None
            for r0 in range(0, kblk, ATTN_ROWS):
                p = jnp.exp(_rows8(lg_scr[h + same, r0:r0 + ATTN_ROWS]) - m_new[None])
                part = jnp.dot(vt_h[:, r0:r0 + ATTN_ROWS],
                               p.reshape(ATTN_ROWS, p.shape[2]).astype(BF16),
                               preferred_element_type=F32)
                pv = part if pv is None else pv + part
            l_scr[h] = alpha * l_scr[h] + pv[HEAD_DIM:HEAD_DIM + SUBLANES]
            acc_scr[h] = (_rows8(acc_scr[h]) * alpha[None]).reshape(acc_scr.shape[1:]) + pv[:HEAD_DIM]
            m_scr[h] = m_new

    def body(kb, c):
        step(kb, *load(kb))
        return c

    lax.fori_loop(0, nkb, body, 0)
    outs = [(_rows8(acc_scr[h]) / l_scr[h][None]).reshape(acc_scr.shape[1:]) for h in range(N_HEADS)]
    return jnp.concatenate(outs, axis=0).T


def _attention_scratch(nkb, tq, kblk):
    return [pltpu.VMEM((nkb, kblk, tq), F32),
            pltpu.VMEM((2, nkb + 1, kblk, tq), I16),
            pltpu.VMEM((nkb, kblk, tq), F32),
            pltpu.VMEM((N_HEADS, kblk, tq), F32),
            pltpu.VMEM((N_HEADS, SUBLANES, tq), F32),
            pltpu.VMEM((N_HEADS, SUBLANES, tq), F32),
            pltpu.VMEM((N_HEADS, SUBLANES, tq), F32),
            pltpu.VMEM((N_HEADS, HEAD_DIM, tq), F32)]


def _pair_blocks(ref, start, rows):
    return [ref[0, pl.ds(start, rows), p * LANES:(p + 1) * LANES] for p in range(PAIRS)]


def _dsa_prompt_kernel(qa_ref, qi_ref, wi_ref, gt_ref, ka_ref, vt_ref, ki_ref, o_ref,
                       sc_scr, half_scr, bias_scr, lg_scr, m_scr, mn_scr, l_scr, acc_scr, *, tq, topk, seq):
    kblk = tq
    j = pl.program_id(1)
    nkb = j + 1
    pos = j * tq + lax.broadcasted_iota(I32, (1, tq), 1)
    limit = _chunk_limit(pos)
    q_heads, w_rows = _index_queries(qi_ref[0], wi_ref[0])

    def score_body(kb, c):
        start = pl.multiple_of(kb * kblk, kblk)
        sc = _block_scores(ki_ref[0, pl.ds(start, kblk), :], q_heads, w_rows)
        kpos = kb * kblk + lax.broadcasted_iota(I32, (kblk, tq), 0)
        sc_scr[kb] = jnp.where(kpos < limit, sc, NEG_INF)
        return c

    lax.fori_loop(0, nkb, score_body, 0)
    _select_bias(sc_scr, half_scr, bias_scr, nkb, limit, topk, seq - nkb * kblk)

    def load(kb):
        return _pair_blocks(ka_ref, pl.multiple_of(kb * kblk, kblk), kblk), vt_ref[0, kb]

    oa = _attention(qa_ref[0], nkb, load, bias_scr, lg_scr, m_scr, mn_scr, l_scr, acc_scr)
    o_ref[0] = (oa * gt_ref[0].astype(F32)).astype(BF16)


def _dsa_prompt(qa, qi, wip, gta, kab, vat, kib, batch, seq):
    tq = KEY_BLOCK
    assert seq % tq == 0 and tq % CHUNK == 0
    topk = min(TOPK_MAX, seq // 4)
    nblk = seq // tq
    r3 = lambda a: a.reshape(batch, seq, a.shape[-1])
    qblk = lambda w: pl.BlockSpec((1, tq, w), lambda b, j: (b, j, 0))
    full = lambda w: pl.BlockSpec((1, seq, w), lambda b, j: (b, 0, 0))
    out = pl.pallas_call(
        functools.partial(_dsa_prompt_kernel, tq=tq, topk=topk, seq=seq),
        grid=(batch, nblk),
        in_specs=[qblk(WIDTH), qblk(WIDTH), qblk(LANES), qblk(WIDTH), full(WIDTH),
                  pl.BlockSpec((1, nblk, WIDTH, tq), lambda b, j: (b, 0, 0, 0)), full(IDX_DIM)],
        out_specs=qblk(WIDTH),
        out_shape=jax.ShapeDtypeStruct((batch, seq, WIDTH), BF16),
        scratch_shapes=_attention_scratch(nblk, tq, tq),
        compiler_params=_params("parallel", "arbitrary"),
    )(r3(qa), r3(qi), r3(wip), r3(gta), r3(kab), vat.reshape(batch, nblk, WIDTH, tq), r3(kib))
    return out.reshape(batch * seq, WIDTH)


def _pad_rows(x, rows):
    return jnp.concatenate([x, jnp.zeros((rows - x.shape[0], x.shape[1]), x.dtype)], axis=0)


def _dsa_sample_kernel(qa_ref, qi_ref, wi_ref, gt_ref, ka_ref, va_ref, ki_ref,
                       ckt_ref, cvt_ref, ckit_ref, o_ref, sc_scr, bias_scr, lg_scr,
                       *, tq, topk, past):
    total = past + tq
    width = past + LANES
    pos = past + lax.broadcasted_iota(I32, (tq, 1), 0)
    limit = jnp.minimum(_chunk_limit(pos), total)
    kpos = lax.broadcasted_iota(I32, (tq, width), 1)

    qi, wi = qi_ref[0], wi_ref[0]
    q_all = jnp.concatenate([qi[:, h * IDX_DIM:(h + 1) * IDX_DIM] for h in range(N_HEADS)], axis=0)
    w_all = jnp.concatenate([jnp.broadcast_to(wi[:, h:h + 1], (tq, LANES)) for h in range(N_HEADS)],
                            axis=0)

    def head_sum(d):
        t = jnp.maximum(d, 0.0) * w_all
        return functools.reduce(jnp.add, [t[h * tq:(h + 1) * tq] for h in range(N_HEADS)])

    kit = ckit_ref[0].astype(BF16)
    tiles = [head_sum(jnp.dot(q_all, kit[:, c:c + LANES], preferred_element_type=F32))
             for c in range(0, past, LANES)]
    tiles.append(head_sum(lax.dot_general(q_all, _pad_rows(ki_ref[0], LANES), _NT,
                                          preferred_element_type=F32)))
    scores = jnp.where(kpos < limit, jnp.concatenate(tiles, axis=1), NEG_INF)
    sc_scr[...] = jnp.where(kpos < total, scores, -jnp.inf)

    def count(pred):
        hit = jnp.where(pred(sc_scr[...]), 1, 0)
        return jnp.sum(functools.reduce(jnp.add, _lane_tiles(hit)), axis=1, keepdims=True)

    def search(i, key):
        cand = key + lax.shift_left(jnp.int32(1), 31 - i)
        cand_f = _key_to_float(cand)
        return jnp.where(count(lambda s: s >= cand_f) >= topk, cand, key)

    thr = _key_to_float(lax.fori_loop(0, 32, search, jnp.full((tq, 1), INT_MIN, I32)))
    c_gt = count(lambda s: s > thr)
    c_eq = count(lambda s: s == thr)
    need = topk - c_gt
    nbits = int(width).bit_length() + 1

    def tie_cut():
        def step(i, cut):
            cand = cut + lax.shift_left(jnp.int32(1), nbits - 1 - i)
            c = count(lambda s: (s == thr) & (kpos < cand))
            return jnp.where(c <= need, cand, cut)
        return lax.fori_loop(0, nbits, step, jnp.zeros((tq, 1), I32))

    excess = jnp.max(jnp.where(c_eq > need, 1, 0)) > 0
    cut = lax.cond(excess, tie_cut, lambda: jnp.full((tq, 1), INT_MAX, I32))
    s = sc_scr[...]
    sel = ((s > thr) | ((s == thr) & (kpos < cut))) & (kpos < limit)
    bias_scr[...] = jnp.where(sel, 0.0, NEG_INF)

    qa, ka_new, va_new = qa_ref[0], ka_ref[0], va_ref[0]
    maxima = []
    for h in range(N_HEADS):
        rows = slice(h * HEAD_DIM, (h + 1) * HEAD_DIM)
        q_h = qa[:, rows]
        k_new = _pad_rows(ka_new[:, rows], LANES)
        lg_c = jnp.dot(q_h, ckt_ref[0, rows, :].astype(BF16), preferred_element_type=F32)
        lg_c = lg_c + bias_scr[:, :past]
        lg_n = lax.dot_general(q_h, k_new, _NT, preferred_element_type=F32) + bias_scr[:, past:]
        lg_scr[h, :, :past] = lg_c
        lg_scr[h, :, past:] = lg_n
        maxima.append(jnp.maximum(jnp.max(lg_c, axis=1, keepdims=True),
                                  jnp.max(lg_n, axis=1, keepdims=True)))
    outs = []
    for h in range(N_HEADS):
        rows = slice(h * HEAD_DIM, (h + 1) * HEAD_DIM)
        v_new = _pad_rows(va_new[:, rows], LANES)
        p = jnp.exp(lg_scr[h] - maxima[h])
        l = jnp.sum(p, axis=1, keepdims=True)
        o = lax.dot_general(p[:, :past].astype(BF16), cvt_ref[0, rows, :].astype(BF16), _NT,
                            preferred_element_type=F32)
        o = o + jnp.dot(p[:, past:].astype(BF16), v_new, preferred_element_type=F32)
        outs.append(o / l)
    oa = jnp.concatenate(outs, axis=1)
    o_ref[0] = (oa * gt_ref[0].astype(F32)).astype(BF16)


def _dsa_sample(qa, qi, wip, gta, kab, vab, kib, cache_kt, cache_vt, cache_kit, batch, tq, past):
    assert past % LANES == 0 and tq <= LANES
    topk = min(TOPK_MAX, (past + tq) // 4)
    assert past + tq >= topk
    r3 = lambda a: a.reshape(batch, tq, a.shape[-1])
    new = lambda w: pl.BlockSpec((1, tq, w), lambda b: (b, 0, 0))
    cache = lambda rows: pl.BlockSpec((1, rows, past), lambda b: (b, 0, 0))
    out = pl.pallas_call(
        functools.partial(_dsa_sample_kernel, tq=tq, topk=topk, past=past),
        grid=(batch,),
        in_specs=[new(WIDTH), new(WIDTH), new(LANES), new(WIDTH),
                  new(WIDTH), new(WIDTH), new(IDX_DIM),
                  cache(WIDTH), cache(WIDTH), cache(IDX_DIM)],
        out_specs=new(WIDTH),
        out_shape=jax.ShapeDtypeStruct((batch, tq, WIDTH), BF16),
        scratch_shapes=[pltpu.VMEM((tq, past + LANES), F32), pltpu.VMEM((tq, past + LANES), F32),
                        pltpu.VMEM((N_HEADS, tq, past + LANES), F32)],
        compiler_params=_params("parallel"),
    )(r3(qa), r3(qi), r3(wip), r3(gta), r3(kab), r3(vab), r3(kib), cache_kt, cache_vt, cache_kit)
    return out.reshape(batch * tq, WIDTH)


SB_DEAD = -104.0


def _sb_step(z_of, pv_of, tri, stage_scr, carry_scr, acc_scr, mask):
    tq = carry_scr.shape[1]
    for h in range(N_HEADS):
        stage_scr[h] = z_of(h)
    live = None
    for h in range(N_HEADS):
        z = stage_scr[h]
        lg = jnp.log(1.0 + jnp.exp(-jnp.abs(z)))
        log1m = -(jnp.maximum(z, 0.0) + lg)
        if mask is not None:
            log1m = jnp.where(mask, log1m, 0.0)
        logsig = jnp.minimum(z, 0.0) - lg
        hi = log1m.astype(BF16)
        lo = (log1m - hi.astype(F32)).astype(BF16)
        after = jnp.dot(jnp.concatenate([hi, lo], axis=1), tri, preferred_element_type=F32)
        carry = carry_scr[h]
        stage_scr[h] = jnp.concatenate([t + carry for t in _lane_tiles(logsig + after)], axis=1)
        carry = carry + jnp.broadcast_to(jnp.sum(log1m, axis=1, keepdims=True), (tq, LANES))
        carry_scr[h] = carry
        live = carry if live is None else jnp.maximum(live, carry)
    for h in range(N_HEADS):
        a = jnp.exp(stage_scr[h])
        if mask is not None:
            a = jnp.where(mask, a, 0.0)
        acc_scr[h] += pv_of(h, a.astype(BF16))
    return (jnp.max(live) > SB_DEAD).astype(I32)


def _sb_pair_fns(qz, k_pairs, v_pairs):
    z_of = lambda h: lax.dot_general(qz[h], k_pairs[h // 2], _NT, preferred_element_type=F32)
    pv_of = lambda h, a: jnp.dot(a, v_pairs[h // 2], preferred_element_type=F32)
    return z_of, pv_of


def _sb_finish(acc_scr, gt):
    lo = _head_masks()
    outs = [jnp.where(lo, acc_scr[2 * p], acc_scr[2 * p + 1]) for p in range(PAIRS)]
    return (jnp.concatenate(outs, axis=1) * gt.astype(F32)).astype(BF16)


def _sb_tri(kblk):
    jj = np.arange(kblk)
    m = (jj[:, None] > jj[None, :])
    return jnp.asarray(np.concatenate([m, m], axis=0), BF16)


def _sb_scratch(tq, kblk):
    return [pltpu.VMEM((N_HEADS, tq, kblk), F32),
            pltpu.VMEM((N_HEADS, tq, LANES), F32),
            pltpu.VMEM((N_HEADS, tq, LANES), F32)]


def _sb_prompt_kernel(qb_ref, gt_ref, kb_ref, vb_ref, tri_ref, o_ref,
                      stage_scr, carry_scr, acc_scr, *, tq):
    j = pl.program_id(1)
    qz = _masked_heads(qb_ref[0])
    tri = tri_ref[...]
    diag = (lax.broadcasted_iota(I32, (tq, tq), 1) < lax.broadcasted_iota(I32, (tq, tq), 0))
    carry_scr[...] = jnp.zeros(carry_scr.shape, F32)
    acc_scr[...] = jnp.zeros(acc_scr.shape, F32)

    def step(kb, mask):
        start = pl.multiple_of(kb * tq, tq)
        fns = _sb_pair_fns(qz, _pair_blocks(kb_ref, start, tq), _pair_blocks(vb_ref, start, tq))
        return _sb_step(*fns, tri, stage_scr, carry_scr, acc_scr, mask)

    live = step(j, diag)
    lax.while_loop(lambda st: (st[0] >= 0) & (st[1] > 0),
                   lambda st: (st[0] - 1, step(st[0], None)), (j - 1, live))
    o_ref[0] = _sb_finish(acc_scr, gt_ref[0])


def _sb_prompt(qb, gtb, kbb, vbb, batch, seq):
    tq = min(SB_TQ, seq)
    assert seq % tq == 0
    r3 = lambda a: a.reshape(batch, seq, a.shape[-1])
    qblk = pl.BlockSpec((1, tq, WIDTH), lambda b, j: (b, j, 0))
    full = pl.BlockSpec((1, seq, WIDTH), lambda b, j: (b, 0, 0))
    out = pl.pallas_call(
        functools.partial(_sb_prompt_kernel, tq=tq),
        grid=(batch, seq // tq),
        in_specs=[qblk, qblk, full, full, pl.BlockSpec((2 * tq, tq), lambda b, j: (0, 0))],
        out_specs=qblk,
        out_shape=jax.ShapeDtypeStruct((batch, seq, WIDTH), BF16),
        scratch_shapes=_sb_scratch(tq, tq),
        compiler_params=_params("parallel", "arbitrary"),
    )(r3(qb), r3(gtb), r3(kbb), r3(vbb), _sb_tri(tq))
    return out.reshape(batch * seq, WIDTH)


def _sb_sample_kernel(qb_ref, gt_ref, kb_ref, vb_ref, ckt_ref, cvt_ref, tri_ref, o_ref,
                      stage_scr, carry_scr, acc_scr, *, tq, kblk, past):
    ncache = past // kblk
    qb = qb_ref[0]
    tri = tri_ref[...]
    diag = (lax.broadcasted_iota(I32, (tq, kblk), 1) < lax.broadcasted_iota(I32, (tq, kblk), 0))
    carry_scr[...] = jnp.zeros(carry_scr.shape, F32)
    acc_scr[...] = jnp.zeros(acc_scr.shape, F32)
    fns = _sb_pair_fns(_masked_heads(qb), [_pad_rows(t, kblk) for t in _lane_tiles(kb_ref[0])],
                       [_pad_rows(t, kblk) for t in _lane_tiles(vb_ref[0])])
    live = _sb_step(*fns, tri, stage_scr, carry_scr, acc_scr, diag)
    q_heads = [qb[:, h * HEAD_DIM:(h + 1) * HEAD_DIM] for h in range(N_HEADS)]
    zeros = jnp.zeros((tq, HEAD_DIM), F32)

    def cache_step(kb):
        cols = slice(kb * kblk, (kb + 1) * kblk)

        def z_of(h):
            kt = ckt_ref[0, h * HEAD_DIM:(h + 1) * HEAD_DIM, cols].astype(BF16)
            return jnp.dot(q_heads[h], kt, preferred_element_type=F32)

        def pv_of(h, a):
            vt = cvt_ref[0, h * HEAD_DIM:(h + 1) * HEAD_DIM, cols].astype(BF16)
            pv = lax.dot_general(a, vt, _NT, preferred_element_type=F32)
            return jnp.concatenate([pv, zeros] if h % 2 == 0 else [zeros, pv], axis=1)

        return _sb_step(z_of, pv_of, tri, stage_scr, carry_scr, acc_scr, None)

    for kb in reversed(range(ncache)):
        live = lax.cond(live > 0, functools.partial(cache_step, kb), lambda: jnp.int32(0))
    o_ref[0] = _sb_finish(acc_scr, gt_ref[0])


def _sb_sample(qb, gtb, kbb, vbb, cache_kt, cache_vt, batch, tq, past):
    kblk = min(KEY_BLOCK, past)
    assert past % kblk == 0 and tq <= kblk
    r3 = lambda a: a.reshape(batch, tq, a.shape[-1])
    new = pl.BlockSpec((1, tq, WIDTH), lambda b: (b, 0, 0))
    cache = pl.BlockSpec((1, WIDTH, past), lambda b: (b, 0, 0))
    out = pl.pallas_call(
        functools.partial(_sb_sample_kernel, tq=tq, kblk=kblk, past=past),
        grid=(batch,),
        in_specs=[new, new, new, new, cache, cache, pl.BlockSpec((2 * kblk, kblk), lambda b: (0, 0))],
        out_specs=new,
        out_shape=jax.ShapeDtypeStruct((batch, tq, WIDTH), BF16),
        scratch_shapes=_sb_scratch(tq, kblk),
        compiler_params=_params("parallel"),
    )(r3(qb), r3(gtb), r3(kbb), r3(vbb), cache_kt, cache_vt, _sb_tri(kblk))
    return out.reshape(batch * tq, WIDTH)


def _merge_kernel(x_ref, ta_ref, tb_ref, sga_ref, sgb_ref, wa_ref, wb_ref, wo_ref, y_ref):
    ya = jnp.dot(ta_ref[...], wa_ref[...], preferred_element_type=F32)
    yb = jnp.dot(tb_ref[...], wb_ref[...], preferred_element_type=F32)
    mixed = sga_ref[...].astype(F32) * ya + sgb_ref[...].astype(F32) * yb
    y_ref[...] = x_ref[...] + jnp.dot(mixed.astype(BF16), wo_ref[...], preferred_element_type=F32)


def _merge(x2d, ta, tb, sga, sgb, wa, wb, wo):
    n, d_model = x2d.shape
    rb = min(MERGE_ROWS, n)
    assert n % rb == 0
    row = lambda w: pl.BlockSpec((rb, w), lambda i: (i, 0))
    const = lambda a: pl.BlockSpec(a.shape, lambda i: (0, 0))
    return pl.pallas_call(
        _merge_kernel,
        grid=(n // rb,),
        in_specs=[row(d_model), row(WIDTH), row(WIDTH), row(d_model), row(d_model),
                  const(wa), const(wb), const(wo)],
        out_specs=row(d_model),
        out_shape=jax.ShapeDtypeStruct((n, d_model), F32),
        compiler_params=_params("parallel"),
    )(x2d, ta, tb, sga, sgb, wa, wb, wo)


def _layer(x, pos, past, params):
    norm_g, w_pack, q_norm_g, k_norm_g, idx_k_norm_g, wa, wb, wo = params
    b, t, d_model = x.shape
    x2d = x.reshape(b * t, d_model)
    (qa, ka, kab, va, vab, gta, qi, ki, kib, wi, qb, kb, kbb, vb, vbb, gtb, sga, sgb, vat) = _project(
        x2d, pos, t, norm_g, w_pack, q_norm_g, k_norm_g, idx_k_norm_g, rows_t=past is None)
    if past is None:
        ta = _dsa_prompt(qa, qi, wi, gta, kab, vat, kib, b, t)
        tb = _sb_prompt(qb, gtb, kbb, vbb, b, t)
    else:
        p_len = past[0].shape[1]
        keys_minor = lambda c: jnp.moveaxis(c, 1, -1).reshape(b, -1, p_len)
        c_ak, c_av, c_ik, c_bk, c_bv = (keys_minor(c) for c in past)
        ta = _dsa_sample(qa, qi, wi, gta, kab, vab, kib, c_ak, c_av, c_ik, b, t, p_len)
        tb = _sb_sample(qb, gtb, kbb, vbb, c_bk, c_bv, b, t, p_len)
    y = _merge(x2d, ta, tb, sga, sgb, wa, wb, wo).reshape(b, t, d_model)
    if past is None:
        heads = lambda a: a.reshape(b, N_HEADS, HEAD_DIM, t).transpose(0, 3, 1, 2)
        rows = (heads(ka), heads(va), ki.transpose(0, 2, 1), heads(kb), heads(vb))
    else:
        heads = lambda a: a.reshape(b, t, N_HEADS, HEAD_DIM)
        rows = (heads(ka), heads(va), ki.reshape(b, t, IDX_DIM), heads(kb), heads(vb))
    return y, rows


def kernel(x_prompt, x_sample, cache_a_k, cache_a_v, cache_idx_k, cache_b_k, cache_b_v,
           norm_g, w_in, q_norm_g, k_norm_g, idx_k_norm_g, w_a_out, w_b_out, w_o):
    depth = norm_g.shape[0]
    d_model = x_prompt.shape[2]
    seq = x_prompt.shape[1]
    past_len = cache_a_k.shape[2]
    dec_seq = x_sample.shape[1]
    pos_p = jnp.arange(seq, dtype=I32)
    pos_s = past_len + jnp.arange(dec_seq, dtype=I32)
    yp, ys = x_prompt, x_sample
    new_p, new_s = [], []
    for l in range(depth):
        params = (norm_g[l], _pack_w_in(w_in[l], d_model), q_norm_g[l], k_norm_g[l], idx_k_norm_g[l],
                  w_a_out[l].astype(BF16), w_b_out[l].astype(BF16), w_o[l].astype(BF16))
        yp, rows_p = _layer(yp, pos_p, None, params)
        past = (cache_a_k[l], cache_a_v[l], cache_idx_k[l], cache_b_k[l], cache_b_v[l])
        ys, rows_s = _layer(ys, pos_s, past, params)
        new_p.append(rows_p)
        new_s.append(rows_s)
    stk = lambda rows, i: jnp.stack([r[i] for r in rows], axis=0)
    return (yp, ys,
            stk(new_p, 0), stk(new_p, 1), stk(new_p, 2), stk(new_p, 3), stk(new_p, 4),
            stk(new_s, 0), stk(new_s, 1), stk(new_s, 2), stk(new_s, 3), stk(new_s, 4))
```

```python
import functools

import numpy as np
import jax
import jax.numpy as jnp
from jax import lax
from jax.experimental import pallas as pl
from jax.experimental.pallas import tpu as pltpu

F32 = jnp.float32
BF16 = jnp.bfloat16
I32 = jnp.int32
I16 = jnp.int16

HEAD_DIM = 64
N_HEADS = 8
WIDTH = N_HEADS * HEAD_DIM
IDX_DIM = 64
CHUNK = 64
TOPK_MAX = 256
ROPE_THETA = 500000.0
EPS = 1e-6
NEG_INF = -1e30

LANES = 128
SUBLANES = 8
PACKED_ROWS = 16
HALF_BIAS = 1 << 15
PAIRS = WIDTH // LANES
VMEM_LIMIT_BYTES = 56 * 1024 * 1024

INT_MIN = -(2 ** 31)
INT_MAX = 2 ** 31 - 1
_NEG_BITS = int(np.array(NEG_INF, np.float32).view(np.int32))
NEG_KEY = _NEG_BITS ^ 0x7FFFFFFF

PROJ_ROWS = 512
MERGE_ROWS = 1024
KEY_BLOCK = 256
SB_TQ = 256
SCORE_ROWS = 128
ATTN_ROWS = 128

_NT = (((1,), (1,)), ((), ()))


def _params(*sem):
    return pltpu.CompilerParams(dimension_semantics=sem, vmem_limit_bytes=VMEM_LIMIT_BYTES)


_G_QA, _G_KA, _G_VA, _G_UA, _G_QI = (i * WIDTH for i in range(5))
_G_KI = 5 * WIDTH
_G_WI = _G_KI + LANES
_G_QB = _G_WI + LANES
_G_KB, _G_VB, _G_UB = (_G_QB + i * WIDTH for i in range(1, 4))
_G_GA = _G_QB + 4 * WIDTH


def _rope(x, cs, s1, s2):
    outs = []
    for j in range(x.shape[1] // LANES):
        xj = x[:, j * LANES:(j + 1) * LANES]
        outs.append(xj * cs + pltpu.roll(xj, LANES - 8, 1) * s1 + pltpu.roll(xj, 8, 1) * s2)
    return outs[0] if len(outs) == 1 else jnp.concatenate(outs, axis=1)


def _proj_kernel(x_ref, g_ref, w_ref, cs_ref, s1_ref, s2_ref, qg_ref, kg_ref, ig_ref, bd_ref,
                 qa_o, ka_o, kab_o, va_o, vab_o, gta_o, qi_o, ki_o, kib_o, wi_o,
                 qb_o, kb_o, kbb_o, vb_o, vbb_o, gtb_o, sga_o, sgb_o, vat_o, *, d_model, rows_t):
    x = x_ref[...]
    ms = jnp.mean(x * x, axis=-1, keepdims=True)
    xn = (x * lax.rsqrt(ms + EPS) * g_ref[...]).astype(BF16)
    cs, s1, s2 = cs_ref[...], s1_ref[...], s2_ref[...]

    def proj(c0, width):
        return jnp.dot(xn, w_ref[:, c0:c0 + width], preferred_element_type=F32)

    def put_row(o_ref, val):
        if rows_t:
            o_ref[0] = val.T
        else:
            o_ref[...] = val

    def head_norm(h, gain):
        ss = jnp.dot((h * h).astype(BF16), bd_ref[...], preferred_element_type=F32)
        return h * lax.rsqrt(ss * (1.0 / HEAD_DIM) + EPS) * gain

    qa = _rope(head_norm(proj(_G_QA, WIDTH), qg_ref[...]), cs, s1, s2)
    qa_o[...] = (qa * (HEAD_DIM ** -0.5)).astype(BF16)
    ka = _rope(head_norm(proj(_G_KA, WIDTH), kg_ref[...]), cs, s1, s2)
    put_row(ka_o, ka)
    kab_o[...] = ka.astype(BF16)
    va = proj(_G_VA, WIDTH)
    put_row(va_o, va)
    vab_o[...] = va.astype(BF16)
    vat = va.T.astype(BF16)
    vt_cols = vat_o.shape[2]
    for c in range(vat_o.shape[0]):
        vat_o[c] = vat[:, c * vt_cols:(c + 1) * vt_cols]
    ua = proj(_G_UA, WIDTH)
    gta_o[...] = (ua * jax.nn.sigmoid(ua)).astype(BF16)
    qi_o[...] = _rope(proj(_G_QI, WIDTH), cs, s1, s2).astype(BF16)

    kw = proj(_G_KI, 2 * LANES)
    hk = kw[:, :LANES]
    ssk = jnp.sum(hk * hk, axis=-1, keepdims=True)
    kin = hk * lax.rsqrt(ssk * (1.0 / IDX_DIM) + EPS) * ig_ref[...]
    ki_pad = _rope(kin, cs, s1, s2)
    ki = ki_pad[:, :IDX_DIM]
    if rows_t:
        ki_o[0] = ki_pad.T[:IDX_DIM]
    else:
        ki_o[...] = ki
    kib_o[...] = ki.astype(BF16)
    wi_o[...] = kw[:, LANES:] * (N_HEADS ** -0.5) * (IDX_DIM ** -0.5)

    qb_o[...] = (proj(_G_QB, WIDTH) * (HEAD_DIM ** -0.5)).astype(BF16)
    kb = proj(_G_KB, WIDTH)
    put_row(kb_o, kb)
    kbb_o[...] = kb.astype(BF16)
    vb = proj(_G_VB, WIDTH)
    put_row(vb_o, vb)
    vbb_o[...] = vb.astype(BF16)
    ub = proj(_G_UB, WIDTH)
    gtb_o[...] = (ub * jax.nn.sigmoid(ub)).astype(BF16)
    sga_o[...] = jax.nn.sigmoid(proj(_G_GA, d_model)).astype(BF16)
    sgb_o[...] = jax.nn.sigmoid(proj(_G_GA + d_model, d_model)).astype(BF16)


def _pack_w_in(w_in, d_model):
    o = 0
    parts = []

    def take(n, pad_to=None):
        nonlocal o
        blk = w_in[:, o:o + n]
        o += n
        if pad_to is not None and pad_to > n:
            blk = jnp.pad(blk, ((0, 0), (0, pad_to - n)))
        parts.append(blk)

    for _ in range(5):
        take(WIDTH)
    take(IDX_DIM, LANES)
    take(N_HEADS, LANES)
    for _ in range(4):
        take(WIDTH)
    take(d_model)
    take(d_model)
    assert o == w_in.shape[1]
    return jnp.concatenate(parts, axis=1).astype(BF16)


def _rope_tables(pos):
    half = HEAD_DIM // 8
    inv_freq = ROPE_THETA ** (-jnp.arange(half, dtype=F32) / half)
    ang = pos.astype(F32)[:, None] * inv_freq[None, :]
    cos, sin = jnp.cos(ang), jnp.sin(ang)
    t = pos.shape[0]
    ones = jnp.ones((t, HEAD_DIM - 2 * half), F32)
    zeros = jnp.zeros((t, HEAD_DIM - 2 * half), F32)
    z8 = jnp.zeros((t, half), F32)
    cs = jnp.concatenate([cos, cos, ones], axis=1)
    s1 = jnp.concatenate([-sin, z8, zeros], axis=1)
    s2 = jnp.concatenate([z8, sin, zeros], axis=1)
    rep = lambda a: jnp.concatenate([a, a], axis=1)
    return rep(cs), rep(s1), rep(s2)


def _project(x2d, pos, t_len, norm_g, w_pack, q_norm_g, k_norm_g, idx_k_norm_g, rows_t):
    n, d_model = x2d.shape
    rb = min(PROJ_ROWS, n)
    vt_cols = min(KEY_BLOCK, rb)
    assert n % rb == 0 and rb % vt_cols == 0
    cs, s1, s2 = _rope_tables(pos)
    if t_len >= rb:
        assert t_len % rb == 0
        tb = t_len // rb
        tab_map = lambda i: (i % tb, 0)
    else:
        assert rb % t_len == 0
        cs, s1, s2 = (jnp.tile(a, (rb // t_len, 1)) for a in (cs, s1, s2))
        tab_map = lambda i: (0, 0)
    qg = jnp.tile(q_norm_g, N_HEADS)[None, :]
    kg = jnp.tile(k_norm_g, N_HEADS)[None, :]
    ig = jnp.pad(idx_k_norm_g, (0, LANES - IDX_DIM))[None, :]
    hid = np.arange(WIDTH) // HEAD_DIM
    bd = jnp.asarray(hid[:, None] == hid[None, :], BF16)
    e_pack = w_pack.shape[1]

    row = lambda w: pl.BlockSpec((rb, w), lambda i: (i, 0))
    const = lambda shape: pl.BlockSpec(shape, lambda i: (0, 0))
    f32o = lambda w: jax.ShapeDtypeStruct((n, w), F32)
    if rows_t:
        assert t_len % rb == 0
        tb = t_len // rb
        rowo = lambda w: (jax.ShapeDtypeStruct((n // t_len, w, t_len), F32),
                          pl.BlockSpec((1, w, rb), lambda i: (i // tb, 0, i % tb)))
    else:
        rowo = lambda w: (f32o(w), row(w))
    b16o = lambda w: jax.ShapeDtypeStruct((n, w), BF16)
    outs = [
        (b16o(WIDTH), row(WIDTH)),
        rowo(WIDTH),
        (b16o(WIDTH), row(WIDTH)),
        rowo(WIDTH),
        (b16o(WIDTH), row(WIDTH)),
        (b16o(WIDTH), row(WIDTH)),
        (b16o(WIDTH), row(WIDTH)),
        rowo(IDX_DIM),
        (b16o(IDX_DIM), row(IDX_DIM)),
        (f32o(LANES), row(LANES)),
        (b16o(WIDTH), row(WIDTH)),
        rowo(WIDTH),
        (b16o(WIDTH), row(WIDTH)),
        rowo(WIDTH),
        (b16o(WIDTH), row(WIDTH)),
        (b16o(WIDTH), row(WIDTH)),
        (b16o(d_model), row(d_model)),
        (b16o(d_model), row(d_model)),
        (jax.ShapeDtypeStruct((n // vt_cols, WIDTH, vt_cols), BF16),
         pl.BlockSpec((rb // vt_cols, WIDTH, vt_cols), lambda i: (i, 0, 0))),
    ]
    return pl.pallas_call(
        functools.partial(_proj_kernel, d_model=d_model, rows_t=rows_t),
        grid=(n // rb,),
        in_specs=[row(d_model), const((1, d_model)),
                  pl.BlockSpec((d_model, e_pack), lambda i: (0, 0), pipeline_mode=pl.Buffered(1)),
                  pl.BlockSpec((rb, LANES), tab_map), pl.BlockSpec((rb, LANES), tab_map),
                  pl.BlockSpec((rb, LANES), tab_map),
                  const((1, WIDTH)), const((1, WIDTH)), const((1, LANES)), const((WIDTH, WIDTH))],
        out_specs=[o[1] for o in outs],
        out_shape=[o[0] for o in outs],
        compiler_params=_params("parallel"),
    )(x2d, norm_g[None, :], w_pack, cs, s1, s2, qg, kg, ig, bd)


def _sort_key(score):
    b = lax.bitcast_convert_type(score, I32)
    b = jnp.where(b == INT_MIN, 0, b)
    return jnp.where(b < 0, b ^ INT_MAX, b)


def _lane_tiles(x):
    return [x[:, c * LANES:(c + 1) * LANES] for c in range(x.shape[1] // LANES)]


def _chunk_limit(pos):
    shift = CHUNK.bit_length() - 1
    assert CHUNK == 1 << shift
    return ((pos >> shift) + 1) << shift


def _head_masks():
    lane = lax.broadcasted_iota(I32, (1, LANES), 1)
    return lane < HEAD_DIM


def _masked_heads(q):
    lo = _head_masks()
    out = []
    for qp in _lane_tiles(q):
        zero = jnp.zeros_like(qp)
        out += [jnp.where(lo, qp, zero), jnp.where(lo, zero, qp)]
    return out


def _transposed(q):
    return q.astype(F32).T.astype(BF16)


def _masked_heads_t(q):
    qt = _transposed(q)
    lo = lax.broadcasted_iota(I32, (LANES, 1), 0) < HEAD_DIM
    out = []
    for p in range(PAIRS):
        qp = qt[p * LANES:(p + 1) * LANES]
        zero = jnp.zeros_like(qp)
        out += [jnp.where(lo, qp, zero), jnp.where(lo, zero, qp)]
    return out


def _rows8(x):
    return x.reshape(x.shape[0] // SUBLANES, SUBLANES, x.shape[1])


def _fold_rows(x, op):
    return op(_rows8(x), axis=0)


def _index_queries(qi, wip):
    qt = _transposed(qi)
    q_heads = [qt[h * IDX_DIM:(h + 1) * IDX_DIM] for h in range(N_HEADS)]
    wt = wip.T
    tq = qi.shape[0]
    return q_heads, [jnp.broadcast_to(wt[h:h + 1, :], (SUBLANES, tq)) for h in range(N_HEADS)]


def _block_scores(ki_blk, q_heads, w_rows):
    parts = []
    for r0 in range(0, ki_blk.shape[0], SCORE_ROWS):
        sc = None
        for h in range(N_HEADS):
            d = jnp.dot(ki_blk[r0:r0 + SCORE_ROWS], q_heads[h], preferred_element_type=F32)
            t = jnp.maximum(_rows8(d), 0.0) * w_rows[h][None]
            sc = t if sc is None else sc + t
        parts.append(sc.reshape(SCORE_ROWS, sc.shape[2]))
    return jnp.concatenate(parts, axis=0)


def _key_to_float(key):
    return lax.bitcast_convert_type(jnp.where(key < 0, key ^ INT_MAX, key), F32)


def _put_scores(sc_scr, half_scr, kb, sc):
    sc_scr[kb] = sc
    k = _sort_key(sc)
    half_scr[0, kb] = (k >> 16).astype(I16)
    half_scr[1, kb] = ((k & 0xFFFF) - HALF_BIAS).astype(I16)


def _select_bias(sc_scr, half_scr, bias_scr, nkb, limit, topk, n_unprocessed):
    _, kblk, tq = sc_scr.shape

    def count(pred):
        def body(kb, acc):
            return acc + _fold_rows(jnp.where(pred(sc_scr[kb], kb), 1, 0), jnp.sum)
        acc = lax.fori_loop(0, nkb, body, jnp.zeros((SUBLANES, tq), I32))
        return jnp.sum(acc, axis=0, keepdims=True)

    def count_ge(t):
        return count(lambda s, kb: s >= t) + jnp.where(t <= NEG_INF, n_unprocessed, 0)

    def count_gt(t):
        return count(lambda s, kb: s > t) + jnp.where(t < NEG_INF, n_unprocessed, 0)

    hi_scr, lo_scr = half_scr.at[0], half_scr.at[1]
    neg_hi, neg_lo = NEG_KEY >> 16, (NEG_KEY & 0xFFFF) - HALF_BIAS

    lowest = jnp.full((kblk, tq), -HALF_BIAS, I16)
    hi_scr[nkb] = lowest
    lo_scr[nkb] = lowest
    npairs = (nkb + 1) // 2

    def count16(scr, cand):
        c16 = cand.astype(I16)

        def body(kp, acc):
            for u in range(2):
                hit = jnp.where(scr[2 * kp + u] >= c16, jnp.int16(1), jnp.int16(0))
                tiles = [hit[r:r + PACKED_ROWS] for r in range(0, kblk, PACKED_ROWS)]
                acc = acc + functools.reduce(jnp.add, tiles)
            return acc
        acc = lax.fori_loop(0, npairs, body, jnp.zeros((PACKED_ROWS, tq), I16))
        return jnp.sum(acc.astype(I32), axis=0, keepdims=True)

    def search16(scr, target, extra):
        def step(i, t):
            cand = t + lax.shift_left(jnp.int32(1), 15 - i)
            return jnp.where(count16(scr, cand) + extra(cand) >= target, cand, t)
        return lax.fori_loop(0, 16, step, jnp.full((1, tq), -HALF_BIAS, I32))

    hi_extra = lambda cand: jnp.where(cand <= neg_hi, n_unprocessed, 0)
    thr_hi = search16(hi_scr, topk, hi_extra)
    above = count16(hi_scr, thr_hi + 1) + hi_extra(thr_hi + 1)
    thr_hi16 = thr_hi.astype(I16)

    def keep_matching(kb, c):
        lo_scr[kb] = jnp.where(hi_scr[kb] == thr_hi16, lo_scr[kb], jnp.int16(-HALF_BIAS))
        return c

    lax.fori_loop(0, nkb, keep_matching, 0)
    lo_extra = lambda cand: jnp.where((thr_hi == neg_hi) & (cand <= neg_lo), n_unprocessed, 0)
    thr_lo = search16(lo_scr, topk - above, lo_extra)
    thr = _key_to_float(lax.shift_left(thr_hi, 16) + (thr_lo + HALF_BIAS))
    c_gt = count_gt(thr)
    c_ge = count_ge(thr)

    def float_search():
        def step(i, key):
            cand = key + lax.shift_left(jnp.int32(1), 31 - i)
            return jnp.where(count_ge(_key_to_float(cand)) >= topk, cand, key)
        t = _key_to_float(lax.fori_loop(0, 32, step, jnp.full((1, tq), INT_MIN, I32)))
        return t, count_gt(t), count_ge(t)

    wrong = jnp.max(jnp.where((c_gt >= topk) | (c_ge < topk), 1, 0)) > 0
    thr, c_gt, c_ge = lax.cond(wrong, float_search, lambda: (thr, c_gt, c_ge))
    need = topk - c_gt
    excess = jnp.max(jnp.where(c_ge - c_gt > need, 1, 0)) > 0

    def key_index(kb):
        return kb * kblk + lax.broadcasted_iota(I32, (kblk, tq), 0)

    nbits = int(sc_scr.shape[0] * kblk).bit_length() + 1

    def tie_cut():
        def step(i, cut):
            cand = cut + lax.shift_left(jnp.int32(1), nbits - 1 - i)
            c = count(lambda s, kb: (s == thr) & (key_index(kb) < cand))
            return jnp.where(c <= need, cand, cut)
        return lax.fori_loop(0, nbits, step, jnp.zeros((1, tq), I32))

    cut = lax.cond(excess, tie_cut, lambda: jnp.full((1, tq), INT_MAX, I32))

    def write(kb, c):
        s = sc_scr[kb]
        idx = key_index(kb)
        sel = ((s > thr) | ((s == thr) & (idx < cut))) & (idx < limit)
        bias_scr[kb] = jnp.where(sel, 0.0, NEG_INF)
        return c

    lax.fori_loop(0, nkb, write, 0)


def _attention(qa, nkb, load, bias_scr, lg_scr, m_scr, mn_scr, l_scr, acc_scr):
    qz = _masked_heads_t(qa)
    kblk = lg_scr.shape[1]
    ones = jnp.ones((PACKED_ROWS, kblk), BF16)
    m_scr[...] = jnp.full(m_scr.shape, NEG_INF, F32)
    l_scr[...] = jnp.zeros(l_scr.shape, F32)
    acc_scr[...] = jnp.zeros(acc_scr.shape, F32)

    def step(kb, k_pairs, vt):
        bias = bias_scr[kb]
        for h in range(N_HEADS):
            lg = jnp.dot(k_pairs[h // 2], qz[h], preferred_element_type=F32) + bias
            lg_scr[h] = lg
            blk_max = jnp.max(_fold_rows(lg, jnp.max), axis=0, keepdims=True)
            mn_scr[h] = jnp.maximum(m_scr[h], blk_max)
        same = jnp.minimum(kb, 0)
        for h in range(N_HEADS):
            m_new = mn_scr[h]
            alpha = jnp.exp(m_scr[h] - m_new)
            vt_h = jnp.concatenate([vt[h * HEAD_DIM:(h + 1) * HEAD_DIM, :], ones], axis=0)
            pv = None
            for r0 in range(0, kblk, ATTN_ROWS):
                p = jnp.exp(_rows8(lg_scr[h + same, r0:r0 + ATTN_ROWS]) - m_new[None])
                part = jnp.dot(vt_h[:, r0:r0 + ATTN_ROWS],
                               p.reshape(ATTN_ROWS, p.shape[2]).astype(BF16),
                               preferred_element_type=F32)
                pv = part if pv is None else pv + part
            l_scr[h] = alpha * l_scr[h] + pv[HEAD_DIM:HEAD_DIM + SUBLANES]
            acc_scr[h] = (_rows8(acc_scr[h]) * alpha[None]).reshape(acc_scr.shape[1:]) + pv[:HEAD_DIM]
            m_scr[h] = m_new

    def body(kb, c):
        step(kb, *load(kb))
        return c

    lax.fori_loop(0, nkb, body, 0)
    outs = [(_rows8(acc_scr[h]) / l_scr[h][None]).reshape(acc_scr.shape[1:]) for h in range(N_HEADS)]
    return jnp.concatenate(outs, axis=0).T


def _attention_scratch(nkb, tq, kblk):
    return [pltpu.VMEM((nkb, kblk, tq), F32),
            pltpu.VMEM((2, nkb + 1, kblk, tq), I16),
            pltpu.VMEM((nkb, kblk, tq), F32),
            pltpu.VMEM((N_HEADS, kblk, tq), F32),
            pltpu.VMEM((N_HEADS, SUBLANES, tq), F32),
            pltpu.VMEM((N_HEADS, SUBLANES, tq), F32),
            pltpu.VMEM((N_HEADS, SUBLANES, tq), F32),
            pltpu.VMEM((N_HEADS, HEAD_DIM, tq), F32)]


def _pair_blocks(ref, start, rows):
    return [ref[0, pl.ds(start, rows), p * LANES:(p + 1) * LANES] for p in range(PAIRS)]


def _dsa_prompt_kernel(qa_ref, qi_ref, wi_ref, gt_ref, ka_ref, vt_ref, ki_ref, o_ref,
                       sc_scr, half_scr, bias_scr, lg_scr, m_scr, mn_scr, l_scr, acc_scr, *, tq, topk, seq):
    kblk = tq
    j = pl.program_id(1)
    nkb = j + 1
    pos = j * tq + lax.broadcasted_iota(I32, (1, tq), 1)
    limit = _chunk_limit(pos)
    q_heads, w_rows = _index_queries(qi_ref[0], wi_ref[0])

    def score_body(kb, c):
        start = pl.multiple_of(kb * kblk, kblk)
        sc = _block_scores(ki_ref[0, pl.ds(start, kblk), :], q_heads, w_rows)
        kpos = kb * kblk + lax.broadcasted_iota(I32, (kblk, tq), 0)
        _put_scores(sc_scr, half_scr, kb, jnp.where(kpos < limit, sc, NEG_INF))
        return c

    lax.fori_loop(0, nkb, score_body, 0)
    _select_bias(sc_scr, half_scr, bias_scr, nkb, limit, topk, seq - nkb * kblk)

    def load(kb):
        return _pair_blocks(ka_ref, pl.multiple_of(kb * kblk, kblk), kblk), vt_ref[0, kb]

    oa = _attention(qa_ref[0], nkb, load, bias_scr, lg_scr, m_scr, mn_scr, l_scr, acc_scr)
    o_ref[0] = (oa * gt_ref[0].astype(F32)).astype(BF16)


def _dsa_prompt(qa, qi, wip, gta, kab, vat, kib, batch, seq):
    tq = KEY_BLOCK
    assert seq % tq == 0 and tq % CHUNK == 0
    topk = min(TOPK_MAX, seq // 4)
    nblk = seq // tq
    r3 = lambda a: a.reshape(batch, seq, a.shape[-1])
    qblk = lambda w: pl.BlockSpec((1, tq, w), lambda b, j: (b, j, 0))
    full = lambda w: pl.BlockSpec((1, seq, w), lambda b, j: (b, 0, 0))
    out = pl.pallas_call(
        functools.partial(_dsa_prompt_kernel, tq=tq, topk=topk, seq=seq),
        grid=(batch, nblk),
        in_specs=[qblk(WIDTH), qblk(WIDTH), qblk(LANES), qblk(WIDTH), full(WIDTH),
                  pl.BlockSpec((1, nblk, WIDTH, tq), lambda b, j: (b, 0, 0, 0)), full(IDX_DIM)],
        out_specs=qblk(WIDTH),
        out_shape=jax.ShapeDtypeStruct((batch, seq, WIDTH), BF16),
        scratch_shapes=_attention_scratch(nblk, tq, tq),
        compiler_params=_params("parallel", "arbitrary"),
    )(r3(qa), r3(qi), r3(wip), r3(gta), r3(kab), vat.reshape(batch, nblk, WIDTH, tq), r3(kib))
    return out.reshape(batch * seq, WIDTH)


def _pad_rows(x, rows):
    return jnp.concatenate([x, jnp.zeros((rows - x.shape[0], x.shape[1]), x.dtype)], axis=0)


def _dsa_sample_kernel(qa_ref, qi_ref, wi_ref, gt_ref, ka_ref, va_ref, ki_ref,
                       ckt_ref, cvt_ref, ckit_ref, o_ref, sc_scr, bias_scr, lg_scr,
                       *, tq, topk, past):
    total = past + tq
    width = past + LANES
    pos = past + lax.broadcasted_iota(I32, (tq, 1), 0)
    limit = jnp.minimum(_chunk_limit(pos), total)
    kpos = lax.broadcasted_iota(I32, (tq, width), 1)

    qi, wi = qi_ref[0], wi_ref[0]
    q_all = jnp.concatenate([qi[:, h * IDX_DIM:(h + 1) * IDX_DIM] for h in range(N_HEADS)], axis=0)
    w_all = jnp.concatenate([jnp.broadcast_to(wi[:, h:h + 1], (tq, LANES)) for h in range(N_HEADS)],
                            axis=0)

    def head_sum(d):
        t = jnp.maximum(d, 0.0) * w_all
        return functools.reduce(jnp.add, [t[h * tq:(h + 1) * tq] for h in range(N_HEADS)])

    kit = ckit_ref[0].astype(BF16)
    tiles = [head_sum(jnp.dot(q_all, kit[:, c:c + LANES], preferred_element_type=F32))
             for c in range(0, past, LANES)]
    tiles.append(head_sum(lax.dot_general(q_all, _pad_rows(ki_ref[0], LANES), _NT,
                                          preferred_element_type=F32)))
    scores = jnp.where(kpos < limit, jnp.concatenate(tiles, axis=1), NEG_INF)
    sc_scr[...] = jnp.where(kpos < total, scores, -jnp.inf)

    def count(pred):
        hit = jnp.where(pred(sc_scr[...]), 1, 0)
        return jnp.sum(functools.reduce(jnp.add, _lane_tiles(hit)), axis=1, keepdims=True)

    def search(i, key):
        cand = key + lax.shift_left(jnp.int32(1), 31 - i)
        cand_f = _key_to_float(cand)
        return jnp.where(count(lambda s: s >= cand_f) >= topk, cand, key)

    thr = _key_to_float(lax.fori_loop(0, 32, search, jnp.full((tq, 1), INT_MIN, I32)))
    c_gt = count(lambda s: s > thr)
    c_eq = count(lambda s: s == thr)
    need = topk - c_gt
    nbits = int(width).bit_length() + 1

    def tie_cut():
        def step(i, cut):
            cand = cut + lax.shift_left(jnp.int32(1), nbits - 1 - i)
            c = count(lambda s: (s == thr) & (kpos < cand))
            return jnp.where(c <= need, cand, cut)
        return lax.fori_loop(0, nbits, step, jnp.zeros((tq, 1), I32))

    excess = jnp.max(jnp.where(c_eq > need, 1, 0)) > 0
    cut = lax.cond(excess, tie_cut, lambda: jnp.full((tq, 1), INT_MAX, I32))
    s = sc_scr[...]
    sel = ((s > thr) | ((s == thr) & (kpos < cut))) & (kpos < limit)
    bias_scr[...] = jnp.where(sel, 0.0, NEG_INF)

    qa, ka_new, va_new = qa_ref[0], ka_ref[0], va_ref[0]
    maxima = []
    for h in range(N_HEADS):
        rows = slice(h * HEAD_DIM, (h + 1) * HEAD_DIM)
        q_h = qa[:, rows]
        k_new = _pad_rows(ka_new[:, rows], LANES)
        lg_c = jnp.dot(q_h, ckt_ref[0, rows, :].astype(BF16), preferred_element_type=F32)
        lg_c = lg_c + bias_scr[:, :past]
        lg_n = lax.dot_general(q_h, k_new, _NT, preferred_element_type=F32) + bias_scr[:, past:]
        lg_scr[h, :, :past] = lg_c
        lg_scr[h, :, past:] = lg_n
        maxima.append(jnp.maximum(jnp.max(lg_c, axis=1, keepdims=True),
                                  jnp.max(lg_n, axis=1, keepdims=True)))
    outs = []
    for h in range(N_HEADS):
        rows = slice(h * HEAD_DIM, (h + 1) * HEAD_DIM)
        v_new = _pad_rows(va_new[:, rows], LANES)
        p = jnp.exp(lg_scr[h] - maxima[h])
        l = jnp.sum(p, axis=1, keepdims=True)
        o = lax.dot_general(p[:, :past].astype(BF16), cvt_ref[0, rows, :].astype(BF16), _NT,
                            preferred_element_type=F32)
        o = o + jnp.dot(p[:, past:].astype(BF16), v_new, preferred_element_type=F32)
        outs.append(o / l)
    oa = jnp.concatenate(outs, axis=1)
    o_ref[0] = (oa * gt_ref[0].astype(F32)).astype(BF16)


def _dsa_sample(qa, qi, wip, gta, kab, vab, kib, cache_kt, cache_vt, cache_kit, batch, tq, past):
    assert past % LANES == 0 and tq <= LANES
    topk = min(TOPK_MAX, (past + tq) // 4)
    assert past + tq >= topk
    r3 = lambda a: a.reshape(batch, tq, a.shape[-1])
    new = lambda w: pl.BlockSpec((1, tq, w), lambda b: (b, 0, 0))
    cache = lambda rows: pl.BlockSpec((1, rows, past), lambda b: (b, 0, 0))
    out = pl.pallas_call(
        functools.partial(_dsa_sample_kernel, tq=tq, topk=topk, past=past),
        grid=(batch,),
        in_specs=[new(WIDTH), new(WIDTH), new(LANES), new(WIDTH),
                  new(WIDTH), new(WIDTH), new(IDX_DIM),
                  cache(WIDTH), cache(WIDTH), cache(IDX_DIM)],
        out_specs=new(WIDTH),
        out_shape=jax.ShapeDtypeStruct((batch, tq, WIDTH), BF16),
        scratch_shapes=[pltpu.VMEM((tq, past + LANES), F32), pltpu.VMEM((tq, past + LANES), F32),
                        pltpu.VMEM((N_HEADS, tq, past + LANES), F32)],
        compiler_params=_params("parallel"),
    )(r3(qa), r3(qi), r3(wip), r3(gta), r3(kab), r3(vab), r3(kib), cache_kt, cache_vt, cache_kit)
    return out.reshape(batch * tq, WIDTH)


SB_DEAD = -104.0


def _sb_step(z_of, pv_of, tri, stage_scr, carry_scr, acc_scr, mask):
    tq = carry_scr.shape[1]
    for h in range(N_HEADS):
        stage_scr[h] = z_of(h)
    live = None
    for h in range(N_HEADS):
        z = stage_scr[h]
        lg = jnp.log(1.0 + jnp.exp(-jnp.abs(z)))
        log1m = -(jnp.maximum(z, 0.0) + lg)
        if mask is not None:
            log1m = jnp.where(mask, log1m, 0.0)
        logsig = jnp.minimum(z, 0.0) - lg
        hi = log1m.astype(BF16)
        lo = (log1m - hi.astype(F32)).astype(BF16)
        after = jnp.dot(jnp.concatenate([hi, lo], axis=1), tri, preferred_element_type=F32)
        carry = carry_scr[h]
        stage_scr[h] = jnp.concatenate([t + carry for t in _lane_tiles(logsig + after)], axis=1)
        carry = carry + jnp.broadcast_to(jnp.sum(log1m, axis=1, keepdims=True), (tq, LANES))
        carry_scr[h] = carry
        live = carry if live is None else jnp.maximum(live, carry)
    for h in range(N_HEADS):
        a = jnp.exp(stage_scr[h])
        if mask is not None:
            a = jnp.where(mask, a, 0.0)
        acc_scr[h] += pv_of(h, a.astype(BF16))
    return (jnp.max(live) > SB_DEAD).astype(I32)


def _sb_pair_fns(qz, k_pairs, v_pairs):
    z_of = lambda h: lax.dot_general(qz[h], k_pairs[h // 2], _NT, preferred_element_type=F32)
    pv_of = lambda h, a: jnp.dot(a, v_pairs[h // 2], preferred_element_type=F32)
    return z_of, pv_of


def _sb_finish(acc_scr, gt):
    lo = _head_masks()
    outs = [jnp.where(lo, acc_scr[2 * p], acc_scr[2 * p + 1]) for p in range(PAIRS)]
    return (jnp.concatenate(outs, axis=1) * gt.astype(F32)).astype(BF16)


def _sb_tri(kblk):
    jj = np.arange(kblk)
    m = (jj[:, None] > jj[None, :])
    return jnp.asarray(np.concatenate([m, m], axis=0), BF16)


def _sb_scratch(tq, kblk):
    return [pltpu.VMEM((N_HEADS, tq, kblk), F32),
            pltpu.VMEM((N_HEADS, tq, LANES), F32),
            pltpu.VMEM((N_HEADS, tq, LANES), F32)]


def _sb_prompt_kernel(qb_ref, gt_ref, kb_ref, vb_ref, tri_ref, o_ref,
                      stage_scr, carry_scr, acc_scr, *, tq):
    j = pl.program_id(1)
    qz = _masked_heads(qb_ref[0])
    tri = tri_ref[...]
    diag = (lax.broadcasted_iota(I32, (tq, tq), 1) < lax.broadcasted_iota(I32, (tq, tq), 0))
    carry_scr[...] = jnp.zeros(carry_scr.shape, F32)
    acc_scr[...] = jnp.zeros(acc_scr.shape, F32)

    def step(kb, mask):
        start = pl.multiple_of(kb * tq, tq)
        fns = _sb_pair_fns(qz, _pair_blocks(kb_ref, start, tq), _pair_blocks(vb_ref, start, tq))
        return _sb_step(*fns, tri, stage_scr, carry_scr, acc_scr, mask)

    live = step(j, diag)
    lax.while_loop(lambda st: (st[0] >= 0) & (st[1] > 0),
                   lambda st: (st[0] - 1, step(st[0], None)), (j - 1, live))
    o_ref[0] = _sb_finish(acc_scr, gt_ref[0])


def _sb_prompt(qb, gtb, kbb, vbb, batch, seq):
    tq = min(SB_TQ, seq)
    assert seq % tq == 0
    r3 = lambda a: a.reshape(batch, seq, a.shape[-1])
    qblk = pl.BlockSpec((1, tq, WIDTH), lambda b, j: (b, j, 0))
    full = pl.BlockSpec((1, seq, WIDTH), lambda b, j: (b, 0, 0))
    out = pl.pallas_call(
        functools.partial(_sb_prompt_kernel, tq=tq),
        grid=(batch, seq // tq),
        in_specs=[qblk, qblk, full, full, pl.BlockSpec((2 * tq, tq), lambda b, j: (0, 0))],
        out_specs=qblk,
        out_shape=jax.ShapeDtypeStruct((batch, seq, WIDTH), BF16),
        scratch_shapes=_sb_scratch(tq, tq),
        compiler_params=_params("parallel", "arbitrary"),
    )(r3(qb), r3(gtb), r3(kbb), r3(vbb), _sb_tri(tq))
    return out.reshape(batch * seq, WIDTH)


def _sb_sample_kernel(qb_ref, gt_ref, kb_ref, vb_ref, ckt_ref, cvt_ref, tri_ref, o_ref,
                      stage_scr, carry_scr, acc_scr, *, tq, kblk, past):
    ncache = past // kblk
    qb = qb_ref[0]
    tri = tri_ref[...]
    diag = (lax.broadcasted_iota(I32, (tq, kblk), 1) < lax.broadcasted_iota(I32, (tq, kblk), 0))
    carry_scr[...] = jnp.zeros(carry_scr.shape, F32)
    acc_scr[...] = jnp.zeros(acc_scr.shape, F32)
    fns = _sb_pair_fns(_masked_heads(qb), [_pad_rows(t, kblk) for t in _lane_tiles(kb_ref[0])],
                       [_pad_rows(t, kblk) for t in _lane_tiles(vb_ref[0])])
    live = _sb_step(*fns, tri, stage_scr, carry_scr, acc_scr, diag)
    q_heads = [qb[:, h * HEAD_DIM:(h + 1) * HEAD_DIM] for h in range(N_HEADS)]
    zeros = jnp.zeros((tq, HEAD_DIM), F32)

    def cache_step(kb):
        cols = slice(kb * kblk, (kb + 1) * kblk)

        def z_of(h):
            kt = ckt_ref[0, h * HEAD_DIM:(h + 1) * HEAD_DIM, cols].astype(BF16)
            return jnp.dot(q_heads[h], kt, preferred_element_type=F32)

        def pv_of(h, a):
            vt = cvt_ref[0, h * HEAD_DIM:(h + 1) * HEAD_DIM, cols].astype(BF16)
            pv = lax.dot_general(a, vt, _NT, preferred_element_type=F32)
            return jnp.concatenate([pv, zeros] if h % 2 == 0 else [zeros, pv], axis=1)

        return _sb_step(z_of, pv_of, tri, stage_scr, carry_scr, acc_scr, None)

    for kb in reversed(range(ncache)):
        live = lax.cond(live > 0, functools.partial(cache_step, kb), lambda: jnp.int32(0))
    o_ref[0] = _sb_finish(acc_scr, gt_ref[0])


def _sb_sample(qb, gtb, kbb, vbb, cache_kt, cache_vt, batch, tq, past):
    kblk = min(KEY_BLOCK, past)
    assert past % kblk == 0 and tq <= kblk
    r3 = lambda a: a.reshape(batch, tq, a.shape[-1])
    new = pl.BlockSpec((1, tq, WIDTH), lambda b: (b, 0, 0))
    cache = pl.BlockSpec((1, WIDTH, past), lambda b: (b, 0, 0))
    out = pl.pallas_call(
        functools.partial(_sb_sample_kernel, tq=tq, kblk=kblk, past=past),
        grid=(batch,),
        in_specs=[new, new, new, new, cache, cache, pl.BlockSpec((2 * kblk, kblk), lambda b: (0, 0))],
        out_specs=new,
        out_shape=jax.ShapeDtypeStruct((batch, tq, WIDTH), BF16),
        scratch_shapes=_sb_scratch(tq, kblk),
        compiler_params=_params("parallel"),
    )(r3(qb), r3(gtb), r3(kbb), r3(vbb), cache_kt, cache_vt, _sb_tri(kblk))
    return out.reshape(batch * tq, WIDTH)


def _merge_kernel(x_ref, ta_ref, tb_ref, sga_ref, sgb_ref, wa_ref, wb_ref, wo_ref, y_ref):
    ya = jnp.dot(ta_ref[...], wa_ref[...], preferred_element_type=F32)
    yb = jnp.dot(tb_ref[...], wb_ref[...], preferred_element_type=F32)
    mixed = sga_ref[...].astype(F32) * ya + sgb_ref[...].astype(F32) * yb
    y_ref[...] = x_ref[...] + jnp.dot(mixed.astype(BF16), wo_ref[...], preferred_element_type=F32)


def _merge(x2d, ta, tb, sga, sgb, wa, wb, wo):
    n, d_model = x2d.shape
    rb = min(MERGE_ROWS, n)
    assert n % rb == 0
    row = lambda w: pl.BlockSpec((rb, w), lambda i: (i, 0))
    const = lambda a: pl.BlockSpec(a.shape, lambda i: (0, 0))
    return pl.pallas_call(
        _merge_kernel,
        grid=(n // rb,),
        in_specs=[row(d_model), row(WIDTH), row(WIDTH), row(d_model), row(d_model),
                  const(wa), const(wb), const(wo)],
        out_specs=row(d_model),
        out_shape=jax.ShapeDtypeStruct((n, d_model), F32),
        compiler_params=_params("parallel"),
    )(x2d, ta, tb, sga, sgb, wa, wb, wo)


def _layer(x, pos, past, params):
    norm_g, w_pack, q_norm_g, k_norm_g, idx_k_norm_g, wa, wb, wo = params
    b, t, d_model = x.shape
    x2d = x.reshape(b * t, d_model)
    (qa, ka, kab, va, vab, gta, qi, ki, kib, wi, qb, kb, kbb, vb, vbb, gtb, sga, sgb, vat) = _project(
        x2d, pos, t, norm_g, w_pack, q_norm_g, k_norm_g, idx_k_norm_g, rows_t=past is None)
    if past is None:
        ta = _dsa_prompt(qa, qi, wi, gta, kab, vat, kib, b, t)
        tb = _sb_prompt(qb, gtb, kbb, vbb, b, t)
    else:
        p_len = past[0].shape[1]
        keys_minor = lambda c: jnp.moveaxis(c, 1, -1).reshape(b, -1, p_len)
        c_ak, c_av, c_ik, c_bk, c_bv = (keys_minor(c) for c in past)
        ta = _dsa_sample(qa, qi, wi, gta, kab, vab, kib, c_ak, c_av, c_ik, b, t, p_len)
        tb = _sb_sample(qb, gtb, kbb, vbb, c_bk, c_bv, b, t, p_len)
    y = _merge(x2d, ta, tb, sga, sgb, wa, wb, wo).reshape(b, t, d_model)
    if past is None:
        heads = lambda a: a.reshape(b, N_HEADS, HEAD_DIM, t).transpose(0, 3, 1, 2)
        rows = (heads(ka), heads(va), ki.transpose(0, 2, 1), heads(kb), heads(vb))
    else:
        heads = lambda a: a.reshape(b, t, N_HEADS, HEAD_DIM)
        rows = (heads(ka), heads(va), ki.reshape(b, t, IDX_DIM), heads(kb), heads(vb))
    return y, rows


def kernel(x_prompt, x_sample, cache_a_k, cache_a_v, cache_idx_k, cache_b_k, cache_b_v,
           norm_g, w_in, q_norm_g, k_norm_g, idx_k_norm_g, w_a_out, w_b_out, w_o):
    depth = norm_g.shape[0]
    d_model = x_prompt.shape[2]
    seq = x_prompt.shape[1]
    past_len = cache_a_k.shape[2]
    dec_seq = x_sample.shape[1]
    pos_p = jnp.arange(seq, dtype=I32)
    pos_s = past_len + jnp.arange(dec_seq, dtype=I32)
    yp, ys = x_prompt, x_sample
    new_p, new_s = [], []
    for l in range(depth):
        params = (norm_g[l], _pack_w_in(w_in[l], d_model), q_norm_g[l], k_norm_g[l], idx_k_norm_g[l],
                  w_a_out[l].astype(BF16), w_b_out[l].astype(BF16), w_o[l].astype(BF16))
        yp, rows_p = _layer(yp, pos_p, None, params)
        past = (cache_a_k[l], cache_a_v[l], cache_idx_k[l], cache_b_k[l], cache_b_v[l])
        ys, rows_s = _layer(ys, pos_s, past, params)
        new_p.append(rows_p)
        new_s.append(rows_s)
    stk = lambda rows, i: jnp.stack([r[i] for r in rows], axis=0)
    return (yp, ys,
            stk(new_p, 0), stk(new_p, 1), stk(new_p, 2), stk(new_p, 3), stk(new_p, 4),
            stk(new_s, 0), stk(new_s, 1), stk(new_s, 2), stk(new_s, 3), stk(new_s, 4))
```
